```python
import math
import jax, jax.numpy as jnp
from jax import lax
import numpy as np

D_MODEL = 1024
BATCH = 8
SEQ = 4096
DEPTH = 2
DEC_BATCH = 32
DEC_SEQ = 8
PAST_LEN = 16384
PAGE_SIZE = 128

N_META = 16
N_HEADS = 8
N_KV_HEADS = 4
GROUP = N_HEADS // N_KV_HEADS
HEAD_DIM = 64
Q_W = N_HEADS * 2 * HEAD_DIM
KV_W = N_KV_HEADS * 2 * HEAD_DIM
ATTN_W = N_HEADS * 2 * HEAD_DIM
D_RNN = 1280
N_LRU_BLOCKS = 10
LRU_BLOCK = D_RNN // N_LRU_BLOCKS
CONV_W = 4
LRU_C = 8.0
D_FF = -(-8 * D_MODEL // (3 * 256)) * 256
N_IN = Q_W + 2 * KV_W + 2 * D_RNN + 2 * D_MODEL
SPLITS = [Q_W, Q_W + KV_W, Q_W + 2 * KV_W, Q_W + 2 * KV_W + D_RNN, Q_W + 2 * KV_W + 2 * D_RNN]
ROPE_THETA = 10000.0
Q_BLOCK = 128
NORM_EPS = 1e-6

kernel_name = "hawk_diffattn_parallel_decoder_step"


def rmsnorm(x, g):
    xf = x.astype(jnp.float32)
    y = xf * lax.rsqrt(jnp.mean(xf * xf, axis=-1, keepdims=True) + NORM_EPS) * g.astype(jnp.float32)
    return y.astype(x.dtype)


def rope_tables(pos):
    inv_freq = 1.0 / (ROPE_THETA ** (jnp.arange(0, HEAD_DIM, 2, dtype=jnp.float32) / HEAD_DIM))
    ang = pos.astype(jnp.float32)[:, None] * inv_freq[None, :]
    ang = jnp.concatenate([ang, ang], axis=-1)
    return jnp.cos(ang), jnp.sin(ang)


def apply_rope(x, cos, sin):
    shape = (1, x.shape[1]) + (1,) * (x.ndim - 3) + (HEAD_DIM,)
    c = cos.reshape(shape)
    s = sin.reshape(shape)
    x1, x2 = jnp.split(x, 2, axis=-1)
    rot = jnp.concatenate([-x2, x1], axis=-1)
    return (x.astype(jnp.float32) * c + rot.astype(jnp.float32) * s).astype(x.dtype)


def diff_attn_core(q, k, v, q_pos, k_pos, k_valid, lam):
    s = jnp.einsum('bqhgcd,bkhcd->bhgcqk', q, k).astype(jnp.float32)
    mask = (k_pos[None, :] <= q_pos[:, None]) & k_valid[None, :]
    s = jnp.where(mask, s, -jnp.inf)
    p = jax.nn.softmax(s, axis=-1)
    a = p[:, :, :, 0] - lam * p[:, :, :, 1]
    return jnp.einsum('bhgqk,bkhe->bqhge', a.astype(v.dtype), v)


def prompt_attention(q, k, v, lam):
    B, T = q.shape[0], q.shape[1]
    n_blk = -(-T // Q_BLOCK)
    t_pad = n_blk * Q_BLOCK
    pad = t_pad - T
    qp = jnp.pad(q, ((0, 0), (0, pad), (0, 0), (0, 0), (0, 0), (0, 0)))
    kp = jnp.pad(k, ((0, 0), (0, pad), (0, 0), (0, 0), (0, 0)))
    vp = jnp.pad(v, ((0, 0), (0, pad), (0, 0), (0, 0)))
    pos_pad = jnp.arange(t_pad, dtype=jnp.int32)
    k_valid = pos_pad < T
    q_blocks = jnp.swapaxes(qp.reshape((B, n_blk, Q_BLOCK) + q.shape[2:]), 0, 1)
    qpos_blocks = pos_pad.reshape(n_blk, Q_BLOCK)

    def one_block(args):
        qb, qposb = args
        return diff_attn_core(qb, kp, vp, qposb, pos_pad, k_valid, lam)

    o = lax.map(one_block, (q_blocks, qpos_blocks))
    o = jnp.swapaxes(o, 0, 1).reshape((B, t_pad) + o.shape[3:])
    return o[:, :T]


def rg_lru(xc, h0, lp):
    B, T, _ = xc.shape
    xb = xc.reshape(B, T, N_LRU_BLOCKS, LRU_BLOCK)
    r = jax.nn.sigmoid(jnp.einsum('btni,nij->btnj', xb, lp['lru_w_a']).reshape(B, T, D_RNN) + lp['lru_b_a'])
    i = jax.nn.sigmoid(jnp.einsum('btni,nij->btnj', xb, lp['lru_w_x']).reshape(B, T, D_RNN) + lp['lru_b_x'])
    log_a = (-LRU_C * jax.nn.softplus(-lp['lru_lambda'].astype(jnp.float32))) * r.astype(jnp.float32)
    a = jnp.exp(log_a)
    b = jnp.sqrt(-jnp.expm1(2.0 * log_a)) * (i * xc).astype(jnp.float32)
    if h0 is not None:
        b = b.at[:, 0].add(a[:, 0] * h0.astype(jnp.float32))

    def combine(lhs, rhs):
        a1, b1 = lhs
        a2, b2 = rhs
        return a1 * a2, a2 * b1 + b2

    _, hs = lax.associative_scan(combine, (a, b), axis=1)
    return hs.astype(xc.dtype), hs[:, -1].astype(xc.dtype)


def decoder_layer(x, pos, lp, lam_init, past_k, past_v, conv_buf, h0):
    B, T, _ = x.shape
    h = rmsnorm(x, lp['norm1_g'])
    z = h @ lp['w_in']
    q, k, v, xr, gr, gates = jnp.split(z, SPLITS, axis=-1)
    q = q.reshape(B, T, N_KV_HEADS, GROUP, 2, HEAD_DIM)
    k = k.reshape(B, T, N_KV_HEADS, 2, HEAD_DIM)
    v = v.reshape(B, T, N_KV_HEADS, 2 * HEAD_DIM)
    cos, sin = rope_tables(pos)
    q = apply_rope(rmsnorm(q, lp['q_norm_g']), cos, sin) * (HEAD_DIM ** -0.5)
    k = apply_rope(rmsnorm(k, lp['k_norm_g']), cos, sin)
    lq = lp['lambda_qk'].astype(jnp.float32)
    lam = jnp.exp(jnp.sum(lq[0] * lq[1])) - jnp.exp(jnp.sum(lq[2] * lq[3])) + lam_init

    if past_k is None:
        o = prompt_attention(q, k, v, lam)
    else:
        k_all = jnp.concatenate([past_k.astype(k.dtype), k], axis=1)
        v_all = jnp.concatenate([past_v.astype(v.dtype), v], axis=1)
        k_pos = jnp.arange(k_all.shape[1], dtype=jnp.int32)
        k_valid = jnp.ones((k_all.shape[1],), dtype=bool)
        o = diff_attn_core(q, k_all, v_all, pos, k_pos, k_valid, lam)
    o = rmsnorm(o, lp['subln_g']) * (1.0 - lam_init)
    y_attn = o.reshape(B, T, ATTN_W) @ lp['w_o_attn']

    if conv_buf is None:
        conv_buf = jnp.zeros((B, CONV_W - 1, D_RNN), dtype=xr.dtype)
    xpad = jnp.concatenate([conv_buf.astype(xr.dtype), xr], axis=1)
    conv_w = lp['conv_w']
    xc = lp['conv_b'] + sum(xpad[:, j:j + T] * conv_w[j] for j in range(CONV_W))
    new_conv = xpad[:, -(CONV_W - 1):]
    hs, h_last = rg_lru(xc, h0, lp)
    y_lru = (hs * jax.nn.gelu(gr)) @ lp['w_o_lru']

    g = jax.nn.sigmoid(gates.reshape(B, T, 2, D_MODEL))
    merged = g[:, :, 0] * y_attn + g[:, :, 1] * y_lru
    x = x + merged @ lp['w_out']

    h2 = rmsnorm(x, lp['norm2_g'])
    gate, up = jnp.split(h2 @ lp['w_gu'], 2, axis=-1)
    x = x + (jax.nn.silu(gate) * up) @ lp['w_down']
    return x, k, v, h_last, new_conv


def setup_inputs(seed: int = 0) -> dict:
    key = jax.random.key(seed)
    ks = jax.random.split(key, 32)
    f32 = jnp.float32
    n_pages = PAST_LEN // PAGE_SIZE
    n_used = DEC_BATCH * n_pages
    n_pool = n_used + max(1, n_used // 4)

    def nrm(k, shape, scale):
        return jax.random.normal(k, shape, f32) * scale

    x_prompt = nrm(ks[0], (BATCH, SEQ, D_MODEL), 1.0)
    x_sample = nrm(ks[1], (DEC_BATCH, DEC_SEQ, D_MODEL), 1.0)
    cache_k = nrm(ks[2], (DEPTH, n_pool, PAGE_SIZE, N_KV_HEADS, 2, HEAD_DIM), 1.0)
    cache_v = nrm(ks[3], (DEPTH, n_pool, PAGE_SIZE, N_KV_HEADS, 2 * HEAD_DIM), 1.0)
    state_h = nrm(ks[4], (DEPTH, DEC_BATCH, D_RNN), 0.5)
    state_conv = nrm(ks[5], (DEPTH, DEC_BATCH, CONV_W - 1, D_RNN), 1.0)
    page_table = jax.random.permutation(ks[6], n_pool)[:n_used].reshape(DEC_BATCH, n_pages).astype(jnp.int32)
    meta_tokens = nrm(ks[7], (N_META, D_MODEL), 1.0)
    norm1_g = 1.0 + nrm(ks[8], (DEPTH, D_MODEL), 0.02)
    w_in = nrm(ks[9], (DEPTH, D_MODEL, N_IN), D_MODEL ** -0.5)
    q_norm_g = 1.0 + nrm(ks[10], (DEPTH, HEAD_DIM), 0.02)
    k_norm_g = 1.0 + nrm(ks[11], (DEPTH, HEAD_DIM), 0.02)
    lambda_qk = nrm(ks[12], (DEPTH, 4, HEAD_DIM), 0.1)
    subln_g = 1.0 + nrm(ks[13], (DEPTH, 2 * HEAD_DIM), 0.02)
    w_o_attn = nrm(ks[14], (DEPTH, ATTN_W, D_MODEL), ATTN_W ** -0.5)
    conv_w = nrm(ks[15], (DEPTH, CONV_W, D_RNN), CONV_W ** -0.5)
    conv_b = nrm(ks[16], (DEPTH, D_RNN), 0.01)
    lru_w_a = nrm(ks[17], (DEPTH, N_LRU_BLOCKS, LRU_BLOCK, LRU_BLOCK), LRU_BLOCK ** -0.5)
    lru_b_a = nrm(ks[18], (DEPTH, D_RNN), 0.01)
    lru_w_x = nrm(ks[19], (DEPTH, N_LRU_BLOCKS, LRU_BLOCK, LRU_BLOCK), LRU_BLOCK ** -0.5)
    lru_b_x = nrm(ks[20], (DEPTH, D_RNN), 0.01)
    u = jax.random.uniform(ks[21], (DEPTH, D_RNN), f32, minval=0.9, maxval=0.999)
    s = u ** (1.0 / LRU_C)
    lru_lambda = jnp.log(s) - jnp.log1p(-s)
    w_o_lru = nrm(ks[22], (DEPTH, D_RNN, D_MODEL), D_RNN ** -0.5)
    w_out = nrm(ks[23], (DEPTH, D_MODEL, D_MODEL), D_MODEL ** -0.5)
    norm2_g = 1.0 + nrm(ks[24], (DEPTH, D_MODEL), 0.02)
    w_gu = nrm(ks[25], (DEPTH, D_MODEL, 2 * D_FF), D_MODEL ** -0.5)
    w_down = nrm(ks[26], (DEPTH, D_FF, D_MODEL), D_FF ** -0.5)
    return {
        'x_prompt': x_prompt, 'x_sample': x_sample,
        'cache_k': cache_k, 'cache_v': cache_v, 'state_h': state_h, 'state_conv': state_conv,
        'page_table': page_table, 'meta_tokens': meta_tokens,
        'norm1_g': norm1_g, 'w_in': w_in, 'q_norm_g': q_norm_g, 'k_norm_g': k_norm_g,
        'lambda_qk': lambda_qk, 'subln_g': subln_g, 'w_o_attn': w_o_attn,
        'conv_w': conv_w, 'conv_b': conv_b, 'lru_w_a': lru_w_a, 'lru_b_a': lru_b_a,
        'lru_w_x': lru_w_x, 'lru_b_x': lru_b_x, 'lru_lambda': lru_lambda, 'w_o_lru': w_o_lru,
        'w_out': w_out, 'norm2_g': norm2_g, 'w_gu': w_gu, 'w_down': w_down,
    }


def reference(x_prompt, x_sample, cache_k, cache_v, state_h, state_conv, page_table, meta_tokens,
              norm1_g, w_in, q_norm_g, k_norm_g, lambda_qk, subln_g, w_o_attn,
              conv_w, conv_b, lru_w_a, lru_b_a, lru_w_x, lru_b_x, lru_lambda, w_o_lru,
              w_out, norm2_g, w_gu, w_down):
    B = x_prompt.shape[0]
    db, n_pages = page_table.shape
    page_size = cache_k.shape[2]
    past_len = n_pages * page_size
    dec_seq = x_sample.shape[1]

    xp = jnp.concatenate([jnp.broadcast_to(meta_tokens.astype(x_prompt.dtype)[None], (B, N_META, D_MODEL)), x_prompt], axis=1)
    pos_p = jnp.arange(xp.shape[1], dtype=jnp.int32)
    xs = x_sample
    pos_s = past_len + jnp.arange(dec_seq, dtype=jnp.int32)

    kp_l, vp_l, hp_l, cp_l = [], [], [], []
    ks_l, vs_l, hs_l, cs_l = [], [], [], []
    for l in range(DEPTH):
        lp = {
            'norm1_g': norm1_g[l], 'w_in': w_in[l], 'q_norm_g': q_norm_g[l], 'k_norm_g': k_norm_g[l],
            'lambda_qk': lambda_qk[l], 'subln_g': subln_g[l], 'w_o_attn': w_o_attn[l],
            'conv_w': conv_w[l], 'conv_b': conv_b[l], 'lru_w_a': lru_w_a[l], 'lru_b_a': lru_b_a[l],
            'lru_w_x': lru_w_x[l], 'lru_b_x': lru_b_x[l], 'lru_lambda': lru_lambda[l], 'w_o_lru': w_o_lru[l],
            'w_out': w_out[l], 'norm2_g': norm2_g[l], 'w_gu': w_gu[l], 'w_down': w_down[l],
        }
        lam_init = 0.8 - 0.6 * math.exp(-0.3 * l)
        xp, k_new, v_new, h_last, conv_new = decoder_layer(xp, pos_p, lp, lam_init, None, None, None, None)
        kp_l.append(k_new); vp_l.append(v_new); hp_l.append(h_last); cp_l.append(conv_new)
        past_k = cache_k[l, page_table].reshape(db, past_len, N_KV_HEADS, 2, HEAD_DIM)
        past_v = cache_v[l, page_table].reshape(db, past_len, N_KV_HEADS, 2 * HEAD_DIM)
        xs, k_new, v_new, h_last, conv_new = decoder_layer(xs, pos_s, lp, lam_init, past_k, past_v, state_conv[l], state_h[l])
        ks_l.append(k_new); vs_l.append(v_new); hs_l.append(h_last); cs_l.append(conv_new)

    y_prompt = xp[:, N_META:]
    y_sample = xs
    return (y_prompt, y_sample,
            jnp.stack(kp_l), jnp.stack(vp_l), jnp.stack(hp_l), jnp.stack(cp_l),
            jnp.stack(ks_l), jnp.stack(vs_l), jnp.stack(hs_l), jnp.stack(cs_l))
```

```python
import functools
import math

import jax
import jax.numpy as jnp
from jax import lax
from jax.experimental import pallas as pl
from jax.experimental.pallas import tpu as pltpu

F32 = jnp.float32
BF16 = jnp.bfloat16

D_MODEL = 1024
N_META = 16
N_KV_HEADS = 4
GROUP = 2
HEAD_DIM = 64
Q_W = 1024
KV_W = 512
D_RNN = 1280
N_LRU_BLOCKS = 10
LRU_BLOCK = 128
CONV_W = 4
LRU_C = 8.0
D_FF = 2816
N_IN = Q_W + 2 * KV_W + 2 * D_RNN + 2 * D_MODEL
ROPE_THETA = 10000.0
NORM_EPS = 1e-6

LANES = 128
SUBLANES = 8
TOK_TILE = 256
ATT_BLOCK = 256
PAGES_PER_STEP = 8
VMEM_LIMIT = 48 * 1024 * 1024


def _cparams(sem):
    return pltpu.CompilerParams(dimension_semantics=sem, vmem_limit_bytes=VMEM_LIMIT)


def _resident(shape):
    return pl.BlockSpec(shape, lambda *_: (0,) * len(shape), pipeline_mode=pl.Buffered(1))


def _rmsnorm_rows(x, g):
    ms = jnp.mean(x * x, axis=-1, keepdims=True)
    return x * lax.rsqrt(ms + NORM_EPS) * g


def _sigmoid(x):
    return 1.0 / (1.0 + jnp.exp(-x))


_IN_SPLITS = ((0, Q_W + KV_W), (Q_W + KV_W, KV_W), (Q_W + 2 * KV_W, D_RNN),
              (Q_W + 2 * KV_W + D_RNN, D_RNN), (Q_W + 2 * KV_W + 2 * D_RNN, 2 * D_MODEL))
_N_CHUNK = 512


def _inproj_kernel(x_ref, g_ref, w_ref, qk_ref, v_ref, xr_ref, gr_ref, gt_ref):
    h = _rmsnorm_rows(x_ref[...], g_ref[...]).astype(BF16)
    for ref, (start, width) in zip((qk_ref, v_ref, xr_ref, gr_ref, gt_ref), _IN_SPLITS):
        for c0 in range(0, width, _N_CHUNK):
            cw = min(_N_CHUNK, width - c0)
            ref[:, c0:c0 + cw] = jnp.dot(h, w_ref[:, start + c0:start + c0 + cw],
                                         preferred_element_type=F32)


def _inproj(x, g, w):
    n = x.shape[0]
    tm = TOK_TILE
    row = lambda width: pl.BlockSpec((tm, width), lambda i: (i, 0))
    return pl.pallas_call(
        _inproj_kernel,
        grid=(n // tm,),
        in_specs=[row(D_MODEL), _resident((1, D_MODEL)), _resident((D_MODEL, N_IN))],
        out_specs=[row(w_) for _, w_ in _IN_SPLITS],
        out_shape=[jax.ShapeDtypeStruct((n, w_), F32) for _, w_ in _IN_SPLITS],
        compiler_params=_cparams(("parallel",)),
        name="inproj",
    )(x, g, w)


def _qkrope_kernel(qk_ref, cos_ref, sin_ref, gq_ref, gk_ref, q_ref, k_ref):
    tm = qk_ref.shape[0]
    lane = lax.broadcasted_iota(jnp.int32, (tm, LANES), 1)
    low = lane < HEAD_DIM
    first_half = (lane & (HEAD_DIM - 1)) < HEAD_DIM // 2
    cos = cos_ref[...]
    sin = sin_ref[...]
    n_q = Q_W // LANES
    for j in range((Q_W + KV_W) // LANES):
        x = qk_ref[:, j * LANES:(j + 1) * LANES]
        ss = x * x
        s_low = jnp.sum(jnp.where(low, ss, 0.0), axis=-1, keepdims=True)
        s_high = jnp.sum(jnp.where(low, 0.0, ss), axis=-1, keepdims=True)
        ms = jnp.where(low, s_low, s_high) * (1.0 / HEAD_DIM)
        g = gq_ref[...] if j < n_q else gk_ref[...]
        y = x * lax.rsqrt(ms + NORM_EPS) * g
        partner = jnp.where(first_half, pltpu.roll(y, LANES - HEAD_DIM // 2, 1),
                            pltpu.roll(y, HEAD_DIM // 2, 1))
        out = y * cos + partner * sin
        if j < n_q:
            q_ref[:, j * LANES:(j + 1) * LANES] = (out * (HEAD_DIM ** -0.5)).astype(BF16)
        else:
            k_ref[:, (j - n_q) * LANES:(j - n_q + 1) * LANES] = out


def _qkrope(qk, cos, sin, gq, gk, table_index):
    n = qk.shape[0]
    tm = TOK_TILE
    tab = pl.BlockSpec((tm, LANES), lambda i: (table_index(i), 0))
    return pl.pallas_call(
        _qkrope_kernel,
        grid=(n // tm,),
        in_specs=[pl.BlockSpec((tm, Q_W + KV_W), lambda i: (i, 0)), tab, tab,
                  _resident((1, LANES)), _resident((1, LANES))],
        out_specs=[pl.BlockSpec((tm, Q_W), lambda i: (i, 0)),
                   pl.BlockSpec((tm, KV_W), lambda i: (i, 0))],
        out_shape=[jax.ShapeDtypeStruct((n, Q_W), BF16),
                   jax.ShapeDtypeStruct((n, KV_W), F32)],
        compiler_params=_cparams(("parallel",)),
        name="qkrope",
    )(qk, cos, sin, gq, gk)


def _lambda_full(lq, lam_init):
    a = jnp.sum(lq[0:1, :] * lq[1:2, :], axis=-1, keepdims=True)
    b = jnp.sum(lq[2:3, :] * lq[3:4, :], axis=-1, keepdims=True)
    return jnp.exp(a) - jnp.exp(b) + lam_init


def _subln(o, g, lam_init):
    return _rmsnorm_rows(o, g) * (1.0 - lam_init)


def _pattn_kernel(q_ref, k_ref, v_ref, lq_ref, sg_ref, o_ref, qs_scr, m_scr, l_scr, acc_scr,
                  *, lam_init):
    tb = ATT_BLOCK
    qi = pl.program_id(2)
    lane = lax.broadcasted_iota(jnp.int32, (tb, LANES), 1)
    low = lane < HEAD_DIM
    for c in range(2):
        for g in range(GROUP):
            qg = q_ref[:, g * LANES:(g + 1) * LANES].astype(F32)
            r0 = (c * GROUP + g) * tb
            sel = jnp.where(low, qg, 0.0) if c == 0 else jnp.where(low, 0.0, qg)
            qs_scr[r0:r0 + tb, :] = sel.astype(BF16)
    m_scr[...] = jnp.full(m_scr.shape, -jnp.inf, F32)
    l_scr[...] = jnp.zeros(l_scr.shape, F32)
    acc_scr[...] = jnp.zeros(acc_scr.shape, F32)

    def step(j, masked):
        start = pl.multiple_of(j * tb, tb)
        kc = k_ref[pl.ds(start, tb), :].astype(BF16)
        vc = v_ref[pl.ds(start, tb), :].astype(BF16)
        s = lax.dot_general(qs_scr[...], kc, (((1,), (1,)), ((), ())),
                            preferred_element_type=F32)
        if masked:
            row = lax.broadcasted_iota(jnp.int32, s.shape, 0)
            col = lax.broadcasted_iota(jnp.int32, s.shape, 1)
            s = jnp.where(col <= (row & (tb - 1)), s, -jnp.inf)
        m_prev = m_scr[...]
        m_new = jnp.maximum(m_prev, jnp.max(s, axis=1, keepdims=True))
        alpha = jnp.exp(m_prev - m_new)
        p = jnp.exp(s - jnp.concatenate([m_new] * (tb // LANES), axis=1))
        l_scr[...] = alpha * l_scr[...] + jnp.sum(p, axis=1, keepdims=True)
        acc_scr[...] = alpha * acc_scr[...] + jnp.dot(p.astype(BF16), vc,
                                                      preferred_element_type=F32)
        m_scr[...] = m_new

    def body(j, carry):
        step(j, False)
        return carry

    lax.fori_loop(0, qi, body, 0)
    step(qi, True)

    on = acc_scr[...] / l_scr[...]
    lam = _lambda_full(lq_ref[...], lam_init)
    half = GROUP * tb
    od = on[0:half] - lam * on[half:2 * half]
    res = _subln(od, sg_ref[...], lam_init).astype(BF16)
    for g in range(GROUP):
        o_ref[:, g * LANES:(g + 1) * LANES] = res[g * tb:(g + 1) * tb]


def _pattn(q, k, v, lq, sg, n_batch, t_pad, lam_init):
    tb = ATT_BLOCK
    nq = t_pad // tb
    n = q.shape[0]
    return pl.pallas_call(
        functools.partial(_pattn_kernel, lam_init=lam_init),
        grid=(n_batch, N_KV_HEADS, nq),
        in_specs=[pl.BlockSpec((tb, GROUP * LANES), lambda b, h, i: (b * nq + i, h)),
                  pl.BlockSpec((t_pad, LANES), lambda b, h, i: (b, h)),
                  pl.BlockSpec((t_pad, LANES), lambda b, h, i: (b, h)),
                  _resident((4, HEAD_DIM)), _resident((1, LANES))],
        out_specs=pl.BlockSpec((tb, GROUP * LANES), lambda b, h, i: (b * nq + i, h)),
        out_shape=jax.ShapeDtypeStruct((n, Q_W), BF16),
        scratch_shapes=[pltpu.VMEM((2 * GROUP * tb, LANES), BF16),
                        pltpu.VMEM((2 * GROUP * tb, LANES), F32),
                        pltpu.VMEM((2 * GROUP * tb, LANES), F32),
                        pltpu.VMEM((2 * GROUP * tb, LANES), F32)],
        compiler_params=_cparams(("parallel", "parallel", "arbitrary")),
        name="pattn",
    )(q, k, v, lq, sg)


_DEC_ROWS = N_KV_HEADS * 2 * GROUP * 8


def _dattn_kernel(pt_ref, qb_ref, *refs, lam_init, n_steps):
    npg = PAGES_PER_STEP
    k_refs = refs[:npg]
    v_refs = refs[npg:2 * npg]
    kn_ref, vn_ref, lq_ref, sg_ref, o_ref, m_scr, l_scr, acc_scr = refs[2 * npg:]
    pc = pl.program_id(1)

    @pl.when(pc == 0)
    def _():
        m_scr[...] = jnp.full(m_scr.shape, -jnp.inf, F32)
        l_scr[...] = jnp.zeros(l_scr.shape, F32)
        acc_scr[...] = jnp.zeros(acc_scr.shape, F32)

    qb = qb_ref[...]

    def update(pages_k, pages_v, masked):
        ss = []
        for kp in pages_k:
            s = lax.dot_general(qb, kp[...].astype(BF16), (((1,), (1,)), ((), ())),
                                preferred_element_type=F32)
            if masked:
                row = lax.broadcasted_iota(jnp.int32, s.shape, 0)
                col = lax.broadcasted_iota(jnp.int32, s.shape, 1)
                s = jnp.where(col <= (row & 7), s, -jnp.inf)
            ss.append(s)
        m_prev = m_scr[...]
        m_cur = ss[0]
        for s in ss[1:]:
            m_cur = jnp.maximum(m_cur, s)
        m_new = jnp.maximum(m_prev, jnp.max(m_cur, axis=1, keepdims=True))
        alpha = jnp.exp(m_prev - m_new)
        l_new = alpha * l_scr[...]
        pv = None
        for s, vp in zip(ss, pages_v):
            p = jnp.exp(s - m_new)
            l_new = l_new + jnp.sum(p, axis=1, keepdims=True)
            d = jnp.dot(p.astype(BF16), vp[...].astype(BF16), preferred_element_type=F32)
            pv = d if pv is None else pv + d
        l_scr[...] = l_new
        acc_scr[...] = jnp.concatenate([alpha] * N_KV_HEADS, axis=1) * acc_scr[...] + pv
        m_scr[...] = m_new

    update(k_refs, v_refs, False)

    @pl.when(pc == n_steps - 1)
    def _():
        update([kn_ref], [vn_ref], True)
        inv = 1.0 / l_scr[...]
        lam = _lambda_full(lq_ref[...], lam_init)
        rows_h = _DEC_ROWS // N_KV_HEADS
        for h in range(N_KV_HEADS):
            blk = (acc_scr[h * rows_h:(h + 1) * rows_h, h * LANES:(h + 1) * LANES]
                   * inv[h * rows_h:(h + 1) * rows_h])
            od = blk[0:rows_h // 2] - lam * blk[rows_h // 2:rows_h]
            o_ref[h * (rows_h // 2):(h + 1) * (rows_h // 2), :] = _subln(od, sg_ref[...], lam_init)


def _dattn(page_table, qbd, cache_k, cache_v, knew, vnew, lq, sg, layer, lam_init):
    db, n_pages = page_table.shape
    npg = PAGES_PER_STEP
    n_steps = n_pages // npg
    page, width = cache_k.shape[2], cache_k.shape[3]

    def page_spec(i):
        return pl.BlockSpec((None, None, page, width),
                            lambda b, pc, pt: (layer, pt[b, pc * npg + i], 0, 0))

    per_b = lambda r, w: pl.BlockSpec((None, r, w), lambda b, pc, pt: (b, 0, 0))
    const = lambda shape: pl.BlockSpec(shape, lambda b, pc, pt: (0,) * len(shape))
    grid_spec = pltpu.PrefetchScalarGridSpec(
        num_scalar_prefetch=1,
        grid=(db, n_steps),
        in_specs=[per_b(_DEC_ROWS, width)] + [page_spec(i) for i in range(npg)]
        + [page_spec(i) for i in range(npg)]
        + [per_b(page, width), per_b(page, width), const((4, HEAD_DIM)), const((1, LANES))],
        out_specs=per_b(_DEC_ROWS // 2, LANES),
        scratch_shapes=[pltpu.VMEM((_DEC_ROWS, LANES), F32), pltpu.VMEM((_DEC_ROWS, LANES), F32),
                        pltpu.VMEM((_DEC_ROWS, width), F32)],
    )
    return pl.pallas_call(
        functools.partial(_dattn_kernel, lam_init=lam_init, n_steps=n_steps),
        grid_spec=grid_spec,
        out_shape=jax.ShapeDtypeStruct((db, _DEC_ROWS // 2, LANES), F32),
        compiler_params=_cparams(("parallel", "arbitrary")),
        name="dattn",
    )(page_table, qbd, *([cache_k] * npg), *([cache_v] * npg), knew, vnew, lq, sg)


def _gelu_tanh(x):
    return 0.5 * x * (1.0 + jnp.tanh(math.sqrt(2.0 / math.pi) * (x + 0.044715 * (x * x * x))))


def _softplus(x):
    return jnp.maximum(x, 0.0) + jnp.log(1.0 + jnp.exp(-jnp.abs(x)))


def _lru_block(x, shifted, gr, h_in, seg_row, seg_len, cw, cb, wax, ba, bx, lam):
    xc = cb + (((shifted[2] * cw[0:1] + shifted[1] * cw[1:2]) + shifted[0] * cw[2:3]) + x * cw[3:4])
    gates = jnp.dot(xc.astype(BF16), wax, preferred_element_type=F32)
    r = _sigmoid(gates[:, :LRU_BLOCK] + ba)
    i = _sigmoid(gates[:, LRU_BLOCK:] + bx)
    log_a = (-LRU_C * _softplus(-lam)) * r
    a = jnp.exp(log_a)
    b = jnp.sqrt(1.0 - jnp.exp(2.0 * log_a)) * (i * xc)
    d = 1
    while d < seg_len:
        keep = seg_row >= d
        a_sh = pltpu.roll(a, d, 0)
        b_sh = pltpu.roll(b, d, 0)
        b = jnp.where(keep, a * b_sh + b, b)
        a = jnp.where(keep, a * a_sh, a)
        d *= 2
    h = a * h_in + b
    return h, h * _gelu_tanh(gr)


def _lru_prompt_kernel(xr_ref, gr_ref, cw_ref, cb_ref, wax_ref, ba_ref, bx_ref, lam_ref,
                       y_ref, hl_ref, cv_ref, xcar, hcar, *, last_tile, last_row):
    tm = xr_ref.shape[0]
    i = pl.program_id(1)

    @pl.when(i == 0)
    def _():
        xcar[...] = jnp.zeros(xcar.shape, F32)
        hcar[...] = jnp.zeros(hcar.shape, F32)

    row = lax.broadcasted_iota(jnp.int32, (tm, LANES), 0)
    row8 = lax.broadcasted_iota(jnp.int32, (SUBLANES, LANES), 0)
    for n in range(N_LRU_BLOCKS):
        cs = slice(n * LANES, (n + 1) * LANES)
        x = xr_ref[:, cs]
        car = xcar[:, cs]
        shifted = []
        for j in range(1, CONV_W):
            main = pltpu.roll(x, j, 0)
            head = jnp.where(row8 < j, pltpu.roll(car, j, 0), main[0:SUBLANES])
            shifted.append(jnp.concatenate([head, main[SUBLANES:]], axis=0))
        h, y = _lru_block(x, shifted, gr_ref[:, cs], hcar[0:1, cs], row, tm,
                          cw_ref[:, cs], cb_ref[:, cs], wax_ref[n], ba_ref[:, cs], bx_ref[:, cs],
                          lam_ref[:, cs])
        y_ref[:, cs] = y.astype(BF16)
        xcar[:, cs] = x[tm - SUBLANES:tm]
        hcar[0:1, cs] = h[tm - 1:tm]

        @pl.when(i == last_tile)
        def _():
            r0 = last_row - (SUBLANES - 1)
            hl_ref[:, cs] = h[r0:r0 + SUBLANES]
            cv_ref[:, cs] = x[r0:r0 + SUBLANES]


def _lru_prompt(xr, gr, cw, cb, wax, ba, bx, lam, n_batch, t_pad, t_real):
    tm = TOK_TILE
    nt = t_pad // tm
    last_tile, last_row = divmod(t_real - 1, tm)
    assert last_row % SUBLANES == SUBLANES - 1
    n = xr.shape[0]
    row = pl.BlockSpec((tm, D_RNN), lambda b, i: (b * nt + i, 0))
    tail = pl.BlockSpec((None, SUBLANES, D_RNN), lambda b, i: (b, 0, 0))
    vec = lambda r: pl.BlockSpec((r, D_RNN), lambda b, i: (0, 0))
    return pl.pallas_call(
        functools.partial(_lru_prompt_kernel, last_tile=last_tile, last_row=last_row),
        grid=(n_batch, nt),
        in_specs=[row, row, vec(CONV_W), vec(1),
                  pl.BlockSpec((N_LRU_BLOCKS, LRU_BLOCK, 2 * LRU_BLOCK), lambda b, i: (0, 0, 0)),
                  vec(1), vec(1), vec(1)],
        out_specs=[row, tail, tail],
        out_shape=[jax.ShapeDtypeStruct((n, D_RNN), BF16),
                   jax.ShapeDtypeStruct((n_batch, SUBLANES, D_RNN), F32),
                   jax.ShapeDtypeStruct((n_batch, SUBLANES, D_RNN), F32)],
        scratch_shapes=[pltpu.VMEM((SUBLANES, D_RNN), F32), pltpu.VMEM((SUBLANES, D_RNN), F32)],
        compiler_params=_cparams(("parallel", "arbitrary")),
        name="lru_prompt",
    )(xr, gr, cw, cb, wax, ba, bx, lam)


def _lru_sample_kernel(xr_ref, gr_ref, p1_ref, p2_ref, p3_ref, h0_ref, cw_ref, cb_ref, wax_ref,
                       ba_ref, bx_ref, lam_ref, y_ref, h_ref, *, seg_len):
    tm = xr_ref.shape[0]
    seg_row = lax.broadcasted_iota(jnp.int32, (tm, LANES), 0) & (seg_len - 1)
    prevs = (p1_ref, p2_ref, p3_ref)
    for n in range(N_LRU_BLOCKS):
        cs = slice(n * LANES, (n + 1) * LANES)
        x = xr_ref[:, cs]
        shifted = [jnp.where(seg_row < j, prevs[j - 1][:, cs], pltpu.roll(x, j, 0))
                   for j in range(1, CONV_W)]
        h, y = _lru_block(x, shifted, gr_ref[:, cs], h0_ref[:, cs], seg_row, seg_len,
                          cw_ref[:, cs], cb_ref[:, cs], wax_ref[n], ba_ref[:, cs], bx_ref[:, cs],
                          lam_ref[:, cs])
        y_ref[:, cs] = y.astype(BF16)
        h_ref[:, cs] = h


def _lru_sample(xr, gr, prevs, h0, cw, cb, wax, ba, bx, lam, seg_len):
    n = xr.shape[0]
    full = lambda shape: pl.BlockSpec(shape, lambda i: (0,) * len(shape))
    return pl.pallas_call(
        functools.partial(_lru_sample_kernel, seg_len=seg_len),
        grid=(1,),
        in_specs=[full((n, D_RNN))] * 6 + [full((CONV_W, D_RNN)), full((1, D_RNN)),
                                            full((N_LRU_BLOCKS, LRU_BLOCK, 2 * LRU_BLOCK)),
                                            full((1, D_RNN)), full((1, D_RNN)), full((1, D_RNN))],
        out_specs=[full((n, D_RNN)), full((n, D_RNN))],
        out_shape=[jax.ShapeDtypeStruct((n, D_RNN), BF16), jax.ShapeDtypeStruct((n, D_RNN), F32)],
        compiler_params=_cparams(("arbitrary",)),
        name="lru_sample",
    )(xr, gr, *prevs, h0, cw, cb, wax, ba, bx, lam)


def _merge_kernel(o_ref, y_ref, gt_ref, x_ref, woa_ref, wol_ref, wout_ref, out_ref):
    ya = jnp.dot(o_ref[...], woa_ref[...], preferred_element_type=F32)
    yl = jnp.dot(y_ref[...], wol_ref[...], preferred_element_type=F32)
    merged = _sigmoid(gt_ref[:, :D_MODEL]) * ya + _sigmoid(gt_ref[:, D_MODEL:]) * yl
    out_ref[...] = x_ref[...] + jnp.dot(merged.astype(BF16), wout_ref[...],
                                        preferred_element_type=F32)


def _merge(o, y, gt, x, woa, wol, wout):
    n = x.shape[0]
    tm = TOK_TILE
    row = lambda width: pl.BlockSpec((tm, width), lambda i: (i, 0))
    return pl.pallas_call(
        _merge_kernel,
        grid=(n // tm,),
        in_specs=[row(Q_W), row(D_RNN), row(2 * D_MODEL), row(D_MODEL),
                  _resident((Q_W, D_MODEL)), _resident((D_RNN, D_MODEL)),
                  _resident((D_MODEL, D_MODEL))],
        out_specs=row(D_MODEL),
        out_shape=jax.ShapeDtypeStruct((n, D_MODEL), F32),
        compiler_params=_cparams(("parallel",)),
        name="merge",
    )(o, y, gt, x, woa, wol, wout)


_FF_CHUNK = 256


def _ffn_kernel(x_ref, g_ref, wgu_ref, wd_ref, out_ref, act_scr):
    x = x_ref[...]
    h = _rmsnorm_rows(x, g_ref[...]).astype(BF16)
    for c0 in range(0, D_FF, _FF_CHUNK):
        gate = jnp.dot(h, wgu_ref[:, c0:c0 + _FF_CHUNK], preferred_element_type=F32)
        up = jnp.dot(h, wgu_ref[:, D_FF + c0:D_FF + c0 + _FF_CHUNK], preferred_element_type=F32)
        act_scr[:, c0:c0 + _FF_CHUNK] = (gate * _sigmoid(gate) * up).astype(BF16)
    out_ref[...] = x + jnp.dot(act_scr[...], wd_ref[...], preferred_element_type=F32)


def _ffn(x, g, wgu, wd):
    n = x.shape[0]
    tm = TOK_TILE
    row = pl.BlockSpec((tm, D_MODEL), lambda i: (i, 0))
    return pl.pallas_call(
        _ffn_kernel,
        grid=(n // tm,),
        in_specs=[row, _resident((1, D_MODEL)), _resident((D_MODEL, 2 * D_FF)),
                  _resident((D_FF, D_MODEL))],
        out_specs=row,
        out_shape=jax.ShapeDtypeStruct((n, D_MODEL), F32),
        scratch_shapes=[pltpu.VMEM((tm, D_FF), BF16)],
        compiler_params=_cparams(("parallel",)),
        name="ffn",
    )(x, g, wgu, wd)


def _rope_tables(pos):
    inv_freq = 1.0 / (ROPE_THETA ** (jnp.arange(0, HEAD_DIM, 2, dtype=F32) / HEAD_DIM))
    ang = pos.astype(F32)[:, None] * inv_freq[None, :]
    cos = jnp.cos(ang)
    sin = jnp.sin(ang)
    return jnp.tile(cos, (1, 4)), jnp.tile(jnp.concatenate([-sin, sin], axis=-1), (1, 2))


def kernel(x_prompt, x_sample, cache_k, cache_v, state_h, state_conv, page_table, meta_tokens,
           norm1_g, w_in, q_norm_g, k_norm_g, lambda_qk, subln_g, w_o_attn,
           conv_w, conv_b, lru_w_a, lru_b_a, lru_w_x, lru_b_x, lru_lambda, w_o_lru,
           w_out, norm2_g, w_gu, w_down):
    n_batch, seq, _ = x_prompt.shape
    db, dec_seq, _ = x_sample.shape
    depth = w_in.shape[0]
    n_pages = page_table.shape[1]
    page = cache_k.shape[2]
    past_len = n_pages * page
    t_real = seq + N_META
    t_pad = -(-t_real // ATT_BLOCK) * ATT_BLOCK
    assert ATT_BLOCK == TOK_TILE and dec_seq == SUBLANES and (db * dec_seq) % TOK_TILE == 0
    assert page == LANES and n_pages % PAGES_PER_STEP == 0

    xp = jnp.concatenate([
        jnp.broadcast_to(meta_tokens.astype(F32)[None], (n_batch, N_META, D_MODEL)), x_prompt,
        jnp.zeros((n_batch, t_pad - t_real, D_MODEL), F32)], axis=1).reshape(n_batch * t_pad, D_MODEL)
    xs = x_sample.reshape(db * dec_seq, D_MODEL)

    cos_p, sin_p = _rope_tables(jnp.arange(t_pad, dtype=jnp.int32))
    pos_s = past_len + (jnp.arange(db * dec_seq, dtype=jnp.int32) % dec_seq)
    cos_s, sin_s = _rope_tables(pos_s)
    tiles_per_seq = t_pad // TOK_TILE

    ck = cache_k.reshape(cache_k.shape[0], cache_k.shape[1], page, KV_W)
    cv = cache_v.reshape(cache_v.shape[0], cache_v.shape[1], page, KV_W)
    eye_h = jnp.eye(N_KV_HEADS, dtype=BF16)
    eye_c = jnp.eye(2, dtype=BF16)

    outs = {k: [] for k in ("kp", "vp", "hp", "cp", "ks", "vs", "hs", "cs")}
    for l in range(depth):
        lam_init = 0.8 - 0.6 * math.exp(-0.3 * l)
        g1 = norm1_g[l][None]
        win = w_in[l].astype(BF16)
        gq = jnp.tile(q_norm_g[l], 2)[None]
        gk = jnp.tile(k_norm_g[l], 2)[None]
        lq = lambda_qk[l]
        sg = subln_g[l][None]
        woa = w_o_attn[l].astype(BF16)
        cw = conv_w[l]
        cb = conv_b[l][None]
        wax = jnp.concatenate([lru_w_a[l], lru_w_x[l]], axis=-1).astype(BF16)
        ba = lru_b_a[l][None]
        bx = lru_b_x[l][None]
        lam_l = lru_lambda[l][None]
        wol = w_o_lru[l].astype(BF16)
        wout = w_out[l].astype(BF16)
        g2 = norm2_g[l][None]
        wgu = w_gu[l].astype(BF16)
        wd = w_down[l].astype(BF16)

        qk, v, xr, gr, gt = _inproj(xp, g1, win)
        q, k = _qkrope(qk, cos_p, sin_p, gq, gk, lambda i: i % tiles_per_seq)
        o = _pattn(q, k, v, lq, sg, n_batch, t_pad, lam_init)
        y, hl, cvn = _lru_prompt(xr, gr, cw, cb, wax, ba, bx, lam_l, n_batch, t_pad, t_real)
        xp = _ffn(_merge(o, y, gt, xp, woa, wol, wout), g2, wgu, wd)
        outs["kp"].append(k.reshape(n_batch, t_pad, N_KV_HEADS, 2, HEAD_DIM)[:, :t_real])
        outs["vp"].append(v.reshape(n_batch, t_pad, N_KV_HEADS, 2 * HEAD_DIM)[:, :t_real])
        outs["hp"].append(hl[:, SUBLANES - 1])
        outs["cp"].append(cvn[:, SUBLANES - (CONV_W - 1):])

        qk, v, xr, gr, gt = _inproj(xs, g1, win)
        q, k = _qkrope(qk, cos_s, sin_s, gq, gk, lambda i: i)
        q6 = q.reshape(db, dec_seq, N_KV_HEADS, GROUP, 2, HEAD_DIM)
        qr = jnp.transpose(q6, (0, 2, 4, 3, 1, 5)).reshape(db, N_KV_HEADS, 2, GROUP * dec_seq, HEAD_DIM)
        qbd = jnp.einsum("bhcrd,hx,cy->bhcrxyd", qr, eye_h, eye_c).reshape(db, _DEC_ROWS, KV_W)
        padk = ((0, 0), (0, page - dec_seq), (0, 0))
        knew = jnp.pad(k.reshape(db, dec_seq, KV_W), padk)
        vnew = jnp.pad(v.reshape(db, dec_seq, KV_W), padk)
        od = _dattn(page_table, qbd, ck, cv, knew, vnew, lq, sg, l, lam_init)
        o = jnp.transpose(od.reshape(db, N_KV_HEADS, GROUP, dec_seq, 2 * HEAD_DIM),
                          (0, 3, 1, 2, 4)).reshape(db * dec_seq, Q_W).astype(BF16)
        conv_ext = jnp.concatenate([state_conv[l], jnp.zeros((db, dec_seq, D_RNN), F32)], axis=1)
        prevs = [conv_ext[:, CONV_W - 1 - j:CONV_W - 1 - j + dec_seq].reshape(db * dec_seq, D_RNN)
                 for j in range(1, CONV_W)]
        h0 = jnp.repeat(state_h[l], dec_seq, axis=0)
        y, hs = _lru_sample(xr, gr, prevs, h0, cw, cb, wax, ba, bx, lam_l, dec_seq)
        xs = _ffn(_merge(o, y, gt, xs, woa, wol, wout), g2, wgu, wd)
        outs["ks"].append(k.reshape(db, dec_seq, N_KV_HEADS, 2, HEAD_DIM))
        outs["vs"].append(v.reshape(db, dec_seq, N_KV_HEADS, 2 * HEAD_DIM))
        outs["hs"].append(hs.reshape(db, dec_seq, D_RNN)[:, -1])
        outs["cs"].append(xr.reshape(db, dec_seq, D_RNN)[:, dec_seq - (CONV_W - 1):])

    y_prompt = xp.reshape(n_batch, t_pad, D_MODEL)[:, N_META:t_real]
    y_sample = xs.reshape(db, dec_seq, D_MODEL)
    return (y_prompt, y_sample,
            jnp.stack(outs["kp"]), jnp.stack(outs["vp"]), jnp.stack(outs["hp"]), jnp.stack(outs["cp"]),
            jnp.stack(outs["ks"]), jnp.stack(outs["vs"]), jnp.stack(outs["hs"]), jnp.stack(outs["cs"]))
```

```python
import functools
import math

import jax
import jax.numpy as jnp
from jax import lax
from jax.experimental import pallas as pl
from jax.experimental.pallas import tpu as pltpu

F32 = jnp.float32
BF16 = jnp.bfloat16

D_MODEL = 1024
N_META = 16
N_KV_HEADS = 4
GROUP = 2
HEAD_DIM = 64
Q_W = 1024
KV_W = 512
D_RNN = 1280
N_LRU_BLOCKS = 10
LRU_BLOCK = 128
CONV_W = 4
LRU_C = 8.0
D_FF = 2816
N_IN = Q_W + 2 * KV_W + 2 * D_RNN + 2 * D_MODEL
ROPE_THETA = 10000.0
NORM_EPS = 1e-6

LANES = 128
SUBLANES = 8
TOK_TILE = 256
ATT_BLOCK = 256
PAGES_PER_STEP = 8
KEY_BLOCKS_PER_CHUNK = 4
VMEM_LIMIT = 48 * 1024 * 1024


def _cparams(sem):
    return pltpu.CompilerParams(dimension_semantics=sem, vmem_limit_bytes=VMEM_LIMIT)


def _resident(shape):
    return pl.BlockSpec(shape, lambda *_: (0,) * len(shape), pipeline_mode=pl.Buffered(1))


def _rmsnorm_rows(x, g):
    ms = jnp.mean(x * x, axis=-1, keepdims=True)
    return x * lax.rsqrt(ms + NORM_EPS) * g


def _sigmoid(x):
    return 1.0 / (1.0 + jnp.exp(-x))


_IN_SPLITS = ((0, Q_W + KV_W), (Q_W + KV_W, KV_W), (Q_W + 2 * KV_W, D_RNN),
              (Q_W + 2 * KV_W + D_RNN, D_RNN), (Q_W + 2 * KV_W + 2 * D_RNN, 2 * D_MODEL))
_N_CHUNK = 512


def _inproj_kernel(x_ref, g_ref, w_ref, qk_ref, v_ref, xr_ref, gr_ref, gt_ref):
    h = _rmsnorm_rows(x_ref[...], g_ref[...]).astype(BF16)
    for ref, (start, width) in zip((qk_ref, v_ref, xr_ref, gr_ref, gt_ref), _IN_SPLITS):
        for c0 in range(0, width, _N_CHUNK):
            cw = min(_N_CHUNK, width - c0)
            ref[:, c0:c0 + cw] = jnp.dot(h, w_ref[:, start + c0:start + c0 + cw],
                                         preferred_element_type=F32)


def _inproj(x, g, w):
    n = x.shape[0]
    tm = TOK_TILE
    row = lambda width: pl.BlockSpec((tm, width), lambda i: (i, 0))
    return pl.pallas_call(
        _inproj_kernel,
        grid=(n // tm,),
        in_specs=[row(D_MODEL), _resident((1, D_MODEL)), _resident((D_MODEL, N_IN))],
        out_specs=[row(w_) for _, w_ in _IN_SPLITS],
        out_shape=[jax.ShapeDtypeStruct((n, w_), F32) for _, w_ in _IN_SPLITS],
        compiler_params=_cparams(("parallel",)),
        name="inproj",
    )(x, g, w)


def _qkrope_tile(qk_ref, cos_ref, sin_ref, gq_ref, gk_ref, q_ref, k_ref, kb_ref):
    tm = qk_ref.shape[0]
    lane = lax.broadcasted_iota(jnp.int32, (tm, LANES), 1)
    low = lane < HEAD_DIM
    first_half = (lane & (HEAD_DIM - 1)) < HEAD_DIM // 2
    cos = cos_ref[...]
    sin = sin_ref[...]
    n_q = Q_W // LANES
    for j in range((Q_W + KV_W) // LANES):
        x = qk_ref[:, j * LANES:(j + 1) * LANES]
        ss = x * x
        s_low = jnp.sum(jnp.where(low, ss, 0.0), axis=-1, keepdims=True)
        s_high = jnp.sum(jnp.where(low, 0.0, ss), axis=-1, keepdims=True)
        ms = jnp.where(low, s_low, s_high) * (1.0 / HEAD_DIM)
        g = gq_ref[...] if j < n_q else gk_ref[...]
        y = x * lax.rsqrt(ms + NORM_EPS) * g
        partner = jnp.where(first_half, pltpu.roll(y, LANES - HEAD_DIM // 2, 1),
                            pltpu.roll(y, HEAD_DIM // 2, 1))
        out = y * cos + partner * sin
        if j < n_q:
            q_ref[:, j * LANES:(j + 1) * LANES] = (out * (HEAD_DIM ** -0.5)).astype(BF16)
        else:
            cs = slice((j - n_q) * LANES, (j - n_q + 1) * LANES)
            k_ref[:, cs] = out
            if kb_ref is not None:
                kb_ref[:, cs] = out.astype(BF16)


def _qkrope_sample_kernel(qk_ref, cos_ref, sin_ref, gq_ref, gk_ref, q_ref, k_ref):
    _qkrope_tile(qk_ref, cos_ref, sin_ref, gq_ref, gk_ref, q_ref, k_ref, None)


def _qkrope_prompt_kernel(qk_ref, v_ref, cos_ref, sin_ref, gq_ref, gk_ref,
                          q_ref, k_ref, kb_ref, vb_ref):
    _qkrope_tile(qk_ref, cos_ref, sin_ref, gq_ref, gk_ref, q_ref, k_ref, kb_ref)
    vb_ref[...] = v_ref[...].astype(BF16)


def _qkrope_sample(qk, cos, sin, gq, gk):
    n = qk.shape[0]
    tm = TOK_TILE
    tab = pl.BlockSpec((tm, LANES), lambda i: (i, 0))
    return pl.pallas_call(
        _qkrope_sample_kernel,
        grid=(n // tm,),
        in_specs=[pl.BlockSpec((tm, Q_W + KV_W), lambda i: (i, 0)), tab, tab,
                  _resident((1, LANES)), _resident((1, LANES))],
        out_specs=[pl.BlockSpec((tm, Q_W), lambda i: (i, 0)),
                   pl.BlockSpec((tm, KV_W), lambda i: (i, 0))],
        out_shape=[jax.ShapeDtypeStruct((n, Q_W), BF16),
                   jax.ShapeDtypeStruct((n, KV_W), F32)],
        compiler_params=_cparams(("parallel",)),
        name="qkrope_sample",
    )(qk, cos, sin, gq, gk)


def _qkrope_prompt(qk, v, cos, sin, gq, gk, n_batch, t_pad):
    n = qk.shape[0]
    tm = TOK_TILE
    nt = t_pad // tm
    tab = pl.BlockSpec((tm, LANES), lambda i: (i % nt, 0))
    row = lambda width: pl.BlockSpec((tm, width), lambda i: (i, 0))
    return pl.pallas_call(
        _qkrope_prompt_kernel,
        grid=(n // tm,),
        in_specs=[row(Q_W + KV_W), row(KV_W), tab, tab,
                  _resident((1, LANES)), _resident((1, LANES))],
        out_specs=[row(Q_W), row(KV_W), row(KV_W), row(KV_W)],
        out_shape=[jax.ShapeDtypeStruct((n, Q_W), BF16),
                   jax.ShapeDtypeStruct((n, KV_W), F32),
                   jax.ShapeDtypeStruct((n, KV_W), BF16),
                   jax.ShapeDtypeStruct((n, KV_W), BF16)],
        compiler_params=_cparams(("parallel",)),
        name="qkrope_prompt",
    )(qk, v, cos, sin, gq, gk)


def _lambda_full(lq, lam_init):
    a = jnp.sum(lq[0:1, :] * lq[1:2, :], axis=-1, keepdims=True)
    b = jnp.sum(lq[2:3, :] * lq[3:4, :], axis=-1, keepdims=True)
    return jnp.exp(a) - jnp.exp(b) + lam_init


def _subln(o, g, lam_init):
    return _rmsnorm_rows(o, g) * (1.0 - lam_init)


def _pattn_kernel(q_ref, kb_ref, vb_ref, lq_ref, sg_ref, o_ref, qs_scr, m_scr, l_scr, acc_scr,
                  *, lam_init):
    tb = ATT_BLOCK
    rows = 2 * GROUP * tb
    qi = pl.program_id(2)
    lane = lax.broadcasted_iota(jnp.int32, (tb, LANES), 1)
    low = lane < HEAD_DIM
    for c in range(2):
        for g in range(GROUP):
            qg = q_ref[:, g * LANES:(g + 1) * LANES].astype(F32)
            r0 = (c * GROUP + g) * tb
            sel = jnp.where(low, qg, 0.0) if c == 0 else jnp.where(low, 0.0, qg)
            qs_scr[r0:r0 + tb, :] = sel.astype(BF16)
    m_scr[...] = jnp.full(m_scr.shape, -jnp.inf, F32)
    l_scr[...] = jnp.zeros(l_scr.shape, F32)
    acc_scr[...] = jnp.zeros(acc_scr.shape, F32)

    def update(start, n_keys, first_visible):
        kc = kb_ref[pl.ds(start, n_keys), :]
        vc = vb_ref[pl.ds(start, n_keys), :]
        s = lax.dot_general(qs_scr[...], kc, (((1,), (1,)), ((), ())),
                            preferred_element_type=F32)
        if first_visible is not None:
            t = lax.broadcasted_iota(jnp.int32, s.shape, 0) & (tb - 1)
            col = lax.broadcasted_iota(jnp.int32, s.shape, 1)
            s = jnp.where(col <= t + first_visible, s, -jnp.inf)
        m_prev = m_scr[...]
        m_new = jnp.maximum(m_prev, jnp.max(s, axis=1, keepdims=True))
        alpha = jnp.exp(m_prev - m_new)
        p = jnp.exp(s - jnp.concatenate([m_new] * (n_keys // LANES), axis=1))
        l_scr[...] = alpha * l_scr[...] + jnp.sum(p, axis=1, keepdims=True)
        acc_scr[...] = alpha * acc_scr[...] + jnp.dot(p.astype(BF16), vc,
                                                      preferred_element_type=F32)
        m_scr[...] = m_new

    nb = KEY_BLOCKS_PER_CHUNK

    def body(j, carry):
        update(pl.multiple_of(j * nb * tb, nb * tb), nb * tb, None)
        return carry

    lax.fori_loop(0, qi // nb, body, 0)

    for r in range(nb):
        @pl.when(qi % nb == r)
        def _(r=r):
            update(pl.multiple_of((qi - r) * tb, tb), (r + 1) * tb, r * tb)

    on = acc_scr[...] / l_scr[...]
    lam = _lambda_full(lq_ref[...], lam_init)
    od = on[0:rows // 2] - lam * on[rows // 2:rows]
    res = _subln(od, sg_ref[...], lam_init).astype(BF16)
    for g in range(GROUP):
        o_ref[:, g * LANES:(g + 1) * LANES] = res[g * tb:(g + 1) * tb]


def _pattn(q, kb, vb, lq, sg, n_batch, t_pad, lam_init):
    tb = ATT_BLOCK
    nq = t_pad // tb
    n = q.shape[0]
    rows = 2 * GROUP * tb
    return pl.pallas_call(
        functools.partial(_pattn_kernel, lam_init=lam_init),
        grid=(n_batch, N_KV_HEADS, nq),
        in_specs=[pl.BlockSpec((tb, GROUP * LANES), lambda b, h, i: (b * nq + i, h)),
                  pl.BlockSpec((t_pad, LANES), lambda b, h, i: (b, h)),
                  pl.BlockSpec((t_pad, LANES), lambda b, h, i: (b, h)),
                  _resident((4, HEAD_DIM)), _resident((1, LANES))],
        out_specs=pl.BlockSpec((tb, GROUP * LANES), lambda b, h, i: (b * nq + i, h)),
        out_shape=jax.ShapeDtypeStruct((n, Q_W), BF16),
        scratch_shapes=[pltpu.VMEM((rows, LANES), BF16),
                        pltpu.VMEM((rows, LANES), F32),
                        pltpu.VMEM((rows, LANES), F32),
                        pltpu.VMEM((rows, LANES), F32)],
        compiler_params=_cparams(("parallel", "parallel", "arbitrary")),
        name="pattn",
    )(q, kb, vb, lq, sg)


_DEC_ROWS = N_KV_HEADS * 2 * GROUP * 8
_ROWS_H = _DEC_ROWS // N_KV_HEADS


def _dattn_kernel(pt_ref, qb_ref, *refs, lam_init, n_steps):
    npg = PAGES_PER_STEP
    k_refs = refs[:npg]
    v_refs = refs[npg:2 * npg]
    kn_ref, vn_ref, lq_ref, sg_ref, o_ref, m_scr, l_scr, acc_scr = refs[2 * npg:]
    pc = pl.program_id(1)

    @pl.when(pc == 0)
    def _():
        m_scr[...] = jnp.full(m_scr.shape, -jnp.inf, F32)
        l_scr[...] = jnp.zeros(l_scr.shape, F32)
        acc_scr[...] = jnp.zeros(acc_scr.shape, F32)

    qb = qb_ref[...]

    def update(pages_k, pages_v, masked):
        ss = []
        for kp in pages_k:
            s = jnp.dot(qb, kp[...].astype(BF16), preferred_element_type=F32)
            if masked:
                row = lax.broadcasted_iota(jnp.int32, s.shape, 0)
                col = lax.broadcasted_iota(jnp.int32, s.shape, 1)
                s = jnp.where(col <= (row & 7), s, -jnp.inf)
            ss.append(s)
        m_prev = m_scr[...]
        m_cur = ss[0]
        for s in ss[1:]:
            m_cur = jnp.maximum(m_cur, s)
        m_new = jnp.maximum(m_prev, jnp.max(m_cur, axis=1, keepdims=True))
        alpha = jnp.exp(m_prev - m_new)
        l_new = alpha * l_scr[...]
        pv = [None] * N_KV_HEADS
        n_keys = pages_v[0].shape[0] // N_KV_HEADS
        for s, vp in zip(ss, pages_v):
            p = jnp.exp(s - m_new)
            l_new = l_new + jnp.sum(p, axis=1, keepdims=True)
            pb = p.astype(BF16)
            for h in range(N_KV_HEADS):
                vh = vp[pl.ds(h, n_keys, stride=N_KV_HEADS), :].astype(BF16)
                d = jnp.dot(pb[h * _ROWS_H:(h + 1) * _ROWS_H], vh, preferred_element_type=F32)
                pv[h] = d if pv[h] is None else pv[h] + d
        l_scr[...] = l_new
        acc_scr[...] = alpha * acc_scr[...] + jnp.concatenate(pv, axis=0)
        m_scr[...] = m_new

    update(k_refs, v_refs, False)

    @pl.when(pc == n_steps - 1)
    def _():
        update([kn_ref], [vn_ref], True)
        on = acc_scr[...] / l_scr[...]
        lam = _lambda_full(lq_ref[...], lam_init)
        for h in range(N_KV_HEADS):
            blk = on[h * _ROWS_H:(h + 1) * _ROWS_H]
            od = blk[0:_ROWS_H // 2] - lam * blk[_ROWS_H // 2:_ROWS_H]
            o_ref[h * (_ROWS_H // 2):(h + 1) * (_ROWS_H // 2), :] = _subln(od, sg_ref[...], lam_init)


def _dattn(page_table, qbd, cache_kt, cache_v2, knew_t, vnew2, lq, sg, layer, lam_init):
    db, n_pages = page_table.shape
    npg = PAGES_PER_STEP
    n_steps = n_pages // npg
    rows, page = cache_kt.shape[2], cache_kt.shape[3]

    def page_spec(i):
        return pl.BlockSpec((None, None, rows, page),
                            lambda b, pc, pt: (layer, pt[b, pc * npg + i], 0, 0))

    per_b = lambda r, w: pl.BlockSpec((None, r, w), lambda b, pc, pt: (b, 0, 0))
    const = lambda shape: pl.BlockSpec(shape, lambda b, pc, pt: (0,) * len(shape))
    grid_spec = pltpu.PrefetchScalarGridSpec(
        num_scalar_prefetch=1,
        grid=(db, n_steps),
        in_specs=[per_b(_DEC_ROWS, rows)] + [page_spec(i) for i in range(npg)]
        + [page_spec(i) for i in range(npg)]
        + [per_b(rows, page), per_b(rows, page), const((4, HEAD_DIM)), const((1, LANES))],
        out_specs=per_b(_DEC_ROWS // 2, LANES),
        scratch_shapes=[pltpu.VMEM((_DEC_ROWS, LANES), F32), pltpu.VMEM((_DEC_ROWS, LANES), F32),
                        pltpu.VMEM((_DEC_ROWS, LANES), F32)],
    )
    return pl.pallas_call(
        functools.partial(_dattn_kernel, lam_init=lam_init, n_steps=n_steps),
        grid_spec=grid_spec,
        out_shape=jax.ShapeDtypeStruct((db, _DEC_ROWS // 2, LANES), F32),
        compiler_params=_cparams(("parallel", "arbitrary")),
        name="dattn",
    )(page_table, qbd, *([cache_kt] * npg), *([cache_v2] * npg), knew_t, vnew2, lq, sg)


def _gelu_tanh(x):
    return 0.5 * x * (1.0 + jnp.tanh(math.sqrt(2.0 / math.pi) * (x + 0.044715 * (x * x * x))))


def _softplus(x):
    return jnp.maximum(x, 0.0) + jnp.log(1.0 + jnp.exp(-jnp.abs(x)))


def _lru_block(x, shifted, gr, h_in, seg_row, seg_len, cw, cb, wax, ba, bx, lam):
    xc = cb + (((shifted[2] * cw[0:1] + shifted[1] * cw[1:2]) + shifted[0] * cw[2:3]) + x * cw[3:4])
    gates = jnp.dot(xc.astype(BF16), wax, preferred_element_type=F32)
    r = _sigmoid(gates[:, :LRU_BLOCK] + ba)
    i = _sigmoid(gates[:, LRU_BLOCK:] + bx)
    log_a = (-LRU_C * _softplus(-lam)) * r
    a = jnp.exp(log_a)
    b = jnp.sqrt(1.0 - jnp.exp(2.0 * log_a)) * (i * xc)
    d = 1
    while d < seg_len:
        keep = seg_row >= d
        a_sh = pltpu.roll(a, d, 0)
        b_sh = pltpu.roll(b, d, 0)
        b = jnp.where(keep, a * b_sh + b, b)
        a = jnp.where(keep, a * a_sh, a)
        d *= 2
    h = a * h_in + b
    return h, h * _gelu_tanh(gr)


def _lru_prompt_kernel(xr_ref, gr_ref, cw_ref, cb_ref, wax_ref, ba_ref, bx_ref, lam_ref,
                       y_ref, hl_ref, cv_ref, xcar, hcar, *, last_tile, last_row):
    tm = xr_ref.shape[0]
    i = pl.program_id(1)

    @pl.when(i == 0)
    def _():
        xcar[...] = jnp.zeros(xcar.shape, F32)
        hcar[...] = jnp.zeros(hcar.shape, F32)

    row = lax.broadcasted_iota(jnp.int32, (tm, LANES), 0)
    row8 = lax.broadcasted_iota(jnp.int32, (SUBLANES, LANES), 0)
    for n in range(N_LRU_BLOCKS):
        cs = slice(n * LANES, (n + 1) * LANES)
        x = xr_ref[:, cs]
        car = xcar[:, cs]
        shifted = []
        for j in range(1, CONV_W):
            main = pltpu.roll(x, j, 0)
            head = jnp.where(row8 < j, pltpu.roll(car, j, 0), main[0:SUBLANES])
            shifted.append(jnp.concatenate([head, main[SUBLANES:]], axis=0))
        h, y = _lru_block(x, shifted, gr_ref[:, cs], hcar[0:1, cs], row, tm,
                          cw_ref[:, cs], cb_ref[:, cs], wax_ref[n], ba_ref[:, cs], bx_ref[:, cs],
                          lam_ref[:, cs])
        y_ref[:, cs] = y.astype(BF16)
        xcar[:, cs] = x[tm - SUBLANES:tm]
        hcar[0:1, cs] = h[tm - 1:tm]

        @pl.when(i == last_tile)
        def _():
            r0 = last_row - (SUBLANES - 1)
            hl_ref[:, cs] = h[r0:r0 + SUBLANES]
            cv_ref[:, cs] = x[r0:r0 + SUBLANES]


def _lru_prompt(xr, gr, cw, cb, wax, ba, bx, lam, n_batch, t_pad, t_real):
    tm = TOK_TILE
    nt = t_pad // tm
    last_tile, last_row = divmod(t_real - 1, tm)
    assert last_row % SUBLANES == SUBLANES - 1
    n = xr.shape[0]
    row = pl.BlockSpec((tm, D_RNN), lambda b, i: (b * nt + i, 0))
    tail = pl.BlockSpec((None, SUBLANES, D_RNN), lambda b, i: (b, 0, 0))
    vec = lambda r: pl.BlockSpec((r, D_RNN), lambda b, i: (0, 0))
    return pl.pallas_call(
        functools.partial(_lru_prompt_kernel, last_tile=last_tile, last_row=last_row),
        grid=(n_batch, nt),
        in_specs=[row, row, vec(CONV_W), vec(1),
                  pl.BlockSpec((N_LRU_BLOCKS, LRU_BLOCK, 2 * LRU_BLOCK), lambda b, i: (0, 0, 0)),
                  vec(1), vec(1), vec(1)],
        out_specs=[row, tail, tail],
        out_shape=[jax.ShapeDtypeStruct((n, D_RNN), BF16),
                   jax.ShapeDtypeStruct((n_batch, SUBLANES, D_RNN), F32),
                   jax.ShapeDtypeStruct((n_batch, SUBLANES, D_RNN), F32)],
        scratch_shapes=[pltpu.VMEM((SUBLANES, D_RNN), F32), pltpu.VMEM((SUBLANES, D_RNN), F32)],
        compiler_params=_cparams(("parallel", "arbitrary")),
        name="lru_prompt",
    )(xr, gr, cw, cb, wax, ba, bx, lam)


def _lru_sample_kernel(xr_ref, gr_ref, p1_ref, p2_ref, p3_ref, h0_ref, cw_ref, cb_ref, wax_ref,
                       ba_ref, bx_ref, lam_ref, y_ref, h_ref, *, seg_len):
    tm = xr_ref.shape[0]
    seg_row = lax.broadcasted_iota(jnp.int32, (tm, LANES), 0) & (seg_len - 1)
    prevs = (p1_ref, p2_ref, p3_ref)
    for n in range(N_LRU_BLOCKS):
        cs = slice(n * LANES, (n + 1) * LANES)
        x = xr_ref[:, cs]
        shifted = [jnp.where(seg_row < j, prevs[j - 1][:, cs], pltpu.roll(x, j, 0))
                   for j in range(1, CONV_W)]
        h, y = _lru_block(x, shifted, gr_ref[:, cs], h0_ref[:, cs], seg_row, seg_len,
                          cw_ref[:, cs], cb_ref[:, cs], wax_ref[n], ba_ref[:, cs], bx_ref[:, cs],
                          lam_ref[:, cs])
        y_ref[:, cs] = y.astype(BF16)
        h_ref[:, cs] = h


def _lru_sample(xr, gr, prevs, h0, cw, cb, wax, ba, bx, lam, seg_len):
    n = xr.shape[0]
    full = lambda shape: pl.BlockSpec(shape, lambda i: (0,) * len(shape))
    return pl.pallas_call(
        functools.partial(_lru_sample_kernel, seg_len=seg_len),
        grid=(1,),
        in_specs=[full((n, D_RNN))] * 6 + [full((CONV_W, D_RNN)), full((1, D_RNN)),
                                            full((N_LRU_BLOCKS, LRU_BLOCK, 2 * LRU_BLOCK)),
                                            full((1, D_RNN)), full((1, D_RNN)), full((1, D_RNN))],
        out_specs=[full((n, D_RNN)), full((n, D_RNN))],
        out_shape=[jax.ShapeDtypeStruct((n, D_RNN), BF16), jax.ShapeDtypeStruct((n, D_RNN), F32)],
        compiler_params=_cparams(("arbitrary",)),
        name="lru_sample",
    )(xr, gr, *prevs, h0, cw, cb, wax, ba, bx, lam)


def _merge_kernel(o_ref, y_ref, gt_ref, x_ref, woa_ref, wol_ref, wout_ref, out_ref):
    ya = jnp.dot(o_ref[...], woa_ref[...], preferred_element_type=F32)
    yl = jnp.dot(y_ref[...], wol_ref[...], preferred_element_type=F32)
    merged = _sigmoid(gt_ref[:, :D_MODEL]) * ya + _sigmoid(gt_ref[:, D_MODEL:]) * yl
    out_ref[...] = x_ref[...] + jnp.dot(merged.astype(BF16), wout_ref[...],
                                        preferred_element_type=F32)


def _merge(o, y, gt, x, woa, wol, wout):
    n = x.shape[0]
    tm = TOK_TILE
    row = lambda width: pl.BlockSpec((tm, width), lambda i: (i, 0))
    return pl.pallas_call(
        _merge_kernel,
        grid=(n // tm,),
        in_specs=[row(Q_W), row(D_RNN), row(2 * D_MODEL), row(D_MODEL),
                  _resident((Q_W, D_MODEL)), _resident((D_RNN, D_MODEL)),
                  _resident((D_MODEL, D_MODEL))],
        out_specs=row(D_MODEL),
        out_shape=jax.ShapeDtypeStruct((n, D_MODEL), F32),
        compiler_params=_cparams(("parallel",)),
        name="merge",
    )(o, y, gt, x, woa, wol, wout)


_FF_CHUNK = 256


def _ffn_kernel(x_ref, g_ref, wgu_ref, wd_ref, out_ref, act_scr):
    x = x_ref[...]
    h = _rmsnorm_rows(x, g_ref[...]).astype(BF16)
    for c0 in range(0, D_FF, _FF_CHUNK):
        gate = jnp.dot(h, wgu_ref[:, c0:c0 + _FF_CHUNK], preferred_element_type=F32)
        up = jnp.dot(h, wgu_ref[:, D_FF + c0:D_FF + c0 + _FF_CHUNK], preferred_element_type=F32)
        act_scr[:, c0:c0 + _FF_CHUNK] = (gate * _sigmoid(gate) * up).astype(BF16)
    out_ref[...] = x + jnp.dot(act_scr[...], wd_ref[...], preferred_element_type=F32)


def _ffn(x, g, wgu, wd):
    n = x.shape[0]
    tm = TOK_TILE
    row = pl.BlockSpec((tm, D_MODEL), lambda i: (i, 0))
    return pl.pallas_call(
        _ffn_kernel,
        grid=(n // tm,),
        in_specs=[row, _resident((1, D_MODEL)), _resident((D_MODEL, 2 * D_FF)),
                  _resident((D_FF, D_MODEL))],
        out_specs=row,
        out_shape=jax.ShapeDtypeStruct((n, D_MODEL), F32),
        scratch_shapes=[pltpu.VMEM((tm, D_FF), BF16)],
        compiler_params=_cparams(("parallel",)),
        name="ffn",
    )(x, g, wgu, wd)


def _rope_tables(pos):
    inv_freq = 1.0 / (ROPE_THETA ** (jnp.arange(0, HEAD_DIM, 2, dtype=F32) / HEAD_DIM))
    ang = pos.astype(F32)[:, None] * inv_freq[None, :]
    cos = jnp.cos(ang)
    sin = jnp.sin(ang)
    return jnp.tile(cos, (1, 4)), jnp.tile(jnp.concatenate([-sin, sin], axis=-1), (1, 2))


def kernel(x_prompt, x_sample, cache_k, cache_v, state_h, state_conv, page_table, meta_tokens,
           norm1_g, w_in, q_norm_g, k_norm_g, lambda_qk, subln_g, w_o_attn,
           conv_w, conv_b, lru_w_a, lru_b_a, lru_w_x, lru_b_x, lru_lambda, w_o_lru,
           w_out, norm2_g, w_gu, w_down):
    n_batch, seq, _ = x_prompt.shape
    db, dec_seq, _ = x_sample.shape
    depth = w_in.shape[0]
    n_pages = page_table.shape[1]
    page = cache_k.shape[2]
    past_len = n_pages * page
    t_real = seq + N_META
    t_pad = -(-t_real // ATT_BLOCK) * ATT_BLOCK
    assert ATT_BLOCK == TOK_TILE and dec_seq == SUBLANES and (db * dec_seq) % TOK_TILE == 0
    assert page == LANES and n_pages % PAGES_PER_STEP == 0

    xp = jnp.concatenate([
        jnp.broadcast_to(meta_tokens.astype(F32)[None], (n_batch, N_META, D_MODEL)), x_prompt,
        jnp.zeros((n_batch, t_pad - t_real, D_MODEL), F32)], axis=1).reshape(n_batch * t_pad, D_MODEL)
    xs = x_sample.reshape(db * dec_seq, D_MODEL)

    cos_p, sin_p = _rope_tables(jnp.arange(t_pad, dtype=jnp.int32))
    pos_s = past_len + (jnp.arange(db * dec_seq, dtype=jnp.int32) % dec_seq)
    cos_s, sin_s = _rope_tables(pos_s)

    ckt = jnp.transpose(cache_k, (0, 1, 3, 4, 5, 2)).reshape(cache_k.shape[0], cache_k.shape[1], KV_W, page)
    cv2 = cache_v.reshape(cache_v.shape[0], cache_v.shape[1], page * N_KV_HEADS, 2 * HEAD_DIM)
    eye_h = jnp.eye(N_KV_HEADS, dtype=BF16)
    eye_c = jnp.eye(2, dtype=BF16)

    outs = {k: [] for k in ("kp", "vp", "hp", "cp", "ks", "vs", "hs", "cs")}
    for l in range(depth):
        lam_init = 0.8 - 0.6 * math.exp(-0.3 * l)
        g1 = norm1_g[l][None]
        win = w_in[l].astype(BF16)
        gq = jnp.tile(q_norm_g[l], 2)[None]
        gk = jnp.tile(k_norm_g[l], 2)[None]
        lq = lambda_qk[l]
        sg = subln_g[l][None]
        woa = w_o_attn[l].astype(BF16)
        cw = conv_w[l]
        cb = conv_b[l][None]
        wax = jnp.concatenate([lru_w_a[l], lru_w_x[l]], axis=-1).astype(BF16)
        ba = lru_b_a[l][None]
        bx = lru_b_x[l][None]
        lam_l = lru_lambda[l][None]
        wol = w_o_lru[l].astype(BF16)
        wout = w_out[l].astype(BF16)
        g2 = norm2_g[l][None]
        wgu = w_gu[l].astype(BF16)
        wd = w_down[l].astype(BF16)

        qk, v, xr, gr, gt = _inproj(xp, g1, win)
        q, k, kb, vb = _qkrope_prompt(qk, v, cos_p, sin_p, gq, gk, n_batch, t_pad)
        o = _pattn(q, kb, vb, lq, sg, n_batch, t_pad, lam_init)
        y, hl, cvn = _lru_prompt(xr, gr, cw, cb, wax, ba, bx, lam_l, n_batch, t_pad, t_real)
        xp = _ffn(_merge(o, y, gt, xp, woa, wol, wout), g2, wgu, wd)
        outs["kp"].append(k.reshape(n_batch, t_pad, N_KV_HEADS, 2, HEAD_DIM)[:, :t_real])
        outs["vp"].append(v.reshape(n_batch, t_pad, N_KV_HEADS, 2 * HEAD_DIM)[:, :t_real])
        outs["hp"].append(hl[:, SUBLANES - 1])
        outs["cp"].append(cvn[:, SUBLANES - (CONV_W - 1):])

        qk, v, xr, gr, gt = _inproj(xs, g1, win)
        q, k = _qkrope_sample(qk, cos_s, sin_s, gq, gk)
        q6 = q.reshape(db, dec_seq, N_KV_HEADS, GROUP, 2, HEAD_DIM)
        qr = jnp.transpose(q6, (0, 2, 4, 3, 1, 5)).reshape(db, N_KV_HEADS, 2, GROUP * dec_seq, HEAD_DIM)
        qbd = jnp.einsum("bhcrd,hx,cy->bhcrxyd", qr, eye_h, eye_c).reshape(db, _DEC_ROWS, KV_W)
        knew_t = jnp.pad(jnp.transpose(k.reshape(db, dec_seq, KV_W), (0, 2, 1)),
                         ((0, 0), (0, 0), (0, page - dec_seq)))
        vnew2 = jnp.pad(v.reshape(db, dec_seq * N_KV_HEADS, 2 * HEAD_DIM),
                        ((0, 0), (0, (page - dec_seq) * N_KV_HEADS), (0, 0)))
        od = _dattn(page_table, qbd, ckt, cv2, knew_t, vnew2, lq, sg, l, lam_init)
        o = jnp.transpose(od.reshape(db, N_KV_HEADS, GROUP, dec_seq, 2 * HEAD_DIM),
                          (0, 3, 1, 2, 4)).reshape(db * dec_seq, Q_W).astype(BF16)
        conv_ext = jnp.concatenate([state_conv[l], jnp.zeros((db, dec_seq, D_RNN), F32)], axis=1)
        prevs = [conv_ext[:, CONV_W - 1 - j:CONV_W - 1 - j + dec_seq].reshape(db * dec_seq, D_RNN)
                 for j in range(1, CONV_W)]
        h0 = jnp.repeat(state_h[l], dec_seq, axis=0)
        y, hs = _lru_sample(xr, gr, prevs, h0, cw, cb, wax, ba, bx, lam_l, dec_seq)
        xs = _ffn(_merge(o, y, gt, xs, woa, wol, wout), g2, wgu, wd)
        outs["ks"].append(k.reshape(db, dec_seq, N_KV_HEADS, 2, HEAD_DIM))
        outs["vs"].append(v.reshape(db, dec_seq, N_KV_HEADS, 2 * HEAD_DIM))
        outs["hs"].append(hs.reshape(db, dec_seq, D_RNN)[:, -1])
        outs["cs"].append(xr.reshape(db, dec_seq, D_RNN)[:, dec_seq - (CONV_W - 1):])

    y_prompt = xp.reshape(n_batch, t_pad, D_MODEL)[:, N_META:t_real]
    y_sample = xs.reshape(db, dec_seq, D_MODEL)
    return (y_prompt, y_sample,
            jnp.stack(outs["kp"]), jnp.stack(outs["vp"]), jnp.stack(outs["hp"]), jnp.stack(outs["cp"]),
            jnp.stack(outs["ks"]), jnp.stack(outs["vs"]), jnp.stack(outs["hs"]), jnp.stack(outs["cs"]))
```

```python
import functools
import math

import jax
import jax.numpy as jnp
from jax import lax
from jax.experimental import pallas as pl
from jax.experimental.pallas import tpu as pltpu

F32 = jnp.float32
BF16 = jnp.bfloat16

D_MODEL = 1024
N_META = 16
N_KV_HEADS = 4
GROUP = 2
HEAD_DIM = 64
Q_W = 1024
KV_W = 512
D_RNN = 1280
N_LRU_BLOCKS = 10
LRU_BLOCK = 128
CONV_W = 4
LRU_C = 8.0
D_FF = 2816
N_IN = Q_W + 2 * KV_W + 2 * D_RNN + 2 * D_MODEL
ROPE_THETA = 10000.0
NORM_EPS = 1e-6

LANES = 128
SUBLANES = 8
TOK_TILE = 256
ATT_BLOCK = 256
PAGES_PER_STEP = 8
LAST_ROW_IN_TILE = 15
KEY_BLOCKS_PER_CHUNK = 4
VMEM_LIMIT = 48 * 1024 * 1024


def _cparams(sem):
    return pltpu.CompilerParams(dimension_semantics=sem, vmem_limit_bytes=VMEM_LIMIT)


def _resident(shape):
    return pl.BlockSpec(shape, lambda *_: (0,) * len(shape), pipeline_mode=pl.Buffered(1))


def _rmsnorm_rows(x, g):
    ms = jnp.mean(x * x, axis=-1, keepdims=True)
    return x * lax.rsqrt(ms + NORM_EPS) * g


def _sigmoid(x):
    return 1.0 / (1.0 + jnp.exp(-x))


_V_START = Q_W + KV_W
_REST_SPLITS = ((Q_W + 2 * KV_W, D_RNN), (Q_W + 2 * KV_W + D_RNN, D_RNN),
                (Q_W + 2 * KV_W + 2 * D_RNN, 2 * D_MODEL))
_N_CHUNK = 512


def _norm_rope(x, cos, sin, g, low, first_half):
    ss = x * x
    s_low = jnp.sum(jnp.where(low, ss, 0.0), axis=-1, keepdims=True)
    s_high = jnp.sum(jnp.where(low, 0.0, ss), axis=-1, keepdims=True)
    ms = jnp.where(low, s_low, s_high) * (1.0 / HEAD_DIM)
    y = x * lax.rsqrt(ms + NORM_EPS) * g
    partner = jnp.where(first_half, pltpu.roll(y, LANES - HEAD_DIM // 2, 1),
                        pltpu.roll(y, HEAD_DIM // 2, 1))
    return y * cos + partner * sin


def _inproj_kernel(x_ref, g_ref, w_ref, cos_ref, sin_ref, gq_ref, gk_ref, *refs, prompt, tiles_per_seq):
    if prompt:
        (cw_ref, cb_ref, wax_ref, ba_ref, bx_ref, lam_ref,
         q_ref, k_ref, kb_ref, v_ref, vb_ref, gt_ref, y_ref, hl_ref, cv_ref,
         xr_ref, gr_ref, xcar, hcar) = refs
    else:
        q_ref, k_ref, v_ref, xr_ref, gr_ref, gt_ref = refs
        kb_ref = vb_ref = None
    tm = x_ref.shape[0]
    h = _rmsnorm_rows(x_ref[...], g_ref[...]).astype(BF16)

    def project(ref, start, width, c0):
        cw = min(_N_CHUNK, width - c0)
        ref[:, c0:c0 + cw] = jnp.dot(h, w_ref[:, start + c0:start + c0 + cw],
                                     preferred_element_type=F32)

    for ref, (start, width) in zip((xr_ref, gr_ref), _REST_SPLITS[:2]):
        for c0 in range(0, width, _N_CHUNK):
            project(ref, start, width, c0)

    lane = lax.broadcasted_iota(jnp.int32, (tm, LANES), 1)
    low = lane < HEAD_DIM
    first_half = (lane & (HEAD_DIM - 1)) < HEAD_DIM // 2
    n_q = Q_W // LANES

    def qk_chunk(c0):
        z = jnp.dot(h, w_ref[:, c0:c0 + _N_CHUNK], preferred_element_type=F32)
        for jb in range(_N_CHUNK // LANES):
            j = c0 // LANES + jb
            out = _norm_rope(z[:, jb * LANES:(jb + 1) * LANES], cos_ref[...], sin_ref[...],
                             gq_ref[...] if j < n_q else gk_ref[...], low, first_half)
            if j < n_q:
                q_ref[:, j * LANES:(j + 1) * LANES] = (out * (HEAD_DIM ** -0.5)).astype(BF16)
            else:
                cs = slice((j - n_q) * LANES, (j - n_q + 1) * LANES)
                k_ref[:, cs] = out
                if kb_ref is not None:
                    kb_ref[:, cs] = out.astype(BF16)

    def v_chunk():
        z = jnp.dot(h, w_ref[:, _V_START:_V_START + KV_W], preferred_element_type=F32)
        v_ref[...] = z
        if vb_ref is not None:
            vb_ref[...] = z.astype(BF16)

    gt_start, gt_width = _REST_SPLITS[2]
    work = [functools.partial(qk_chunk, c0) for c0 in range(0, Q_W + KV_W, _N_CHUNK)] + [v_chunk]
    work += [functools.partial(project, gt_ref, gt_start, gt_width, c0)
             for c0 in range(0, gt_width, _N_CHUNK)]
    if not prompt:
        for item in work:
            item()
        return

    first_tile = pl.program_id(0) % tiles_per_seq == 0
    row = lax.broadcasted_iota(jnp.int32, (tm, LANES), 0)
    row8 = lax.broadcasted_iota(jnp.int32, (SUBLANES, LANES), 0)
    for n in range(N_LRU_BLOCKS):
        cs = slice(n * LANES, (n + 1) * LANES)
        x = xr_ref[:, cs]
        car = jnp.where(first_tile, 0.0, xcar[:, cs])
        h_in = jnp.where(first_tile, 0.0, hcar[0:1, cs])
        shifted = []
        for j in range(1, CONV_W):
            main = pltpu.roll(x, j, 0)
            head = jnp.where(row8 < j, pltpu.roll(car, j, 0), main[0:SUBLANES])
            shifted.append(jnp.concatenate([head, main[SUBLANES:]], axis=0))
        hs, y = _lru_block(x, shifted, gr_ref[:, cs], h_in, row, tm,
                           cw_ref[:, cs], cb_ref[:, cs], wax_ref[n], ba_ref[:, cs], bx_ref[:, cs],
                           lam_ref[:, cs])
        y_ref[:, cs] = y.astype(BF16)
        xcar[:, cs] = x[tm - SUBLANES:tm]
        hcar[0:1, cs] = hs[tm - 1:tm]
        r0 = LAST_ROW_IN_TILE - (SUBLANES - 1)
        hl_ref[:, cs] = hs[r0:r0 + SUBLANES]
        cv_ref[:, cs] = x[r0:r0 + SUBLANES]
        if n < len(work):
            work[n]()
    for item in work[N_LRU_BLOCKS:]:
        item()


def _inproj_prompt(x, g, w, cos, sin, gq, gk, cw, cb, wax, ba, bx, lam, n_batch, t_pad, t_real):
    n = x.shape[0]
    tm = TOK_TILE
    nt = t_pad // tm
    last_tile, last_row = divmod(t_real - 1, tm)
    assert last_tile == nt - 1 and last_row == LAST_ROW_IN_TILE
    row = lambda width: pl.BlockSpec((tm, width), lambda i: (i, 0))
    tab = pl.BlockSpec((tm, LANES), lambda i: (i % nt, 0))
    tail = pl.BlockSpec((None, SUBLANES, D_RNN), lambda i: (i // nt, 0, 0))
    outs = [(Q_W, BF16), (KV_W, F32), (KV_W, BF16), (KV_W, F32), (KV_W, BF16), (2 * D_MODEL, F32),
            (D_RNN, BF16)]
    return pl.pallas_call(
        functools.partial(_inproj_kernel, prompt=True, tiles_per_seq=nt),
        grid=(n // tm,),
        in_specs=[row(D_MODEL), _resident((1, D_MODEL)), _resident((D_MODEL, N_IN)), tab, tab,
                  _resident((1, LANES)), _resident((1, LANES)),
                  _resident((CONV_W, D_RNN)), _resident((1, D_RNN)),
                  _resident((N_LRU_BLOCKS, LRU_BLOCK, 2 * LRU_BLOCK)),
                  _resident((1, D_RNN)), _resident((1, D_RNN)), _resident((1, D_RNN))],
        out_specs=[row(w_) for w_, _ in outs] + [tail, tail],
        out_shape=[jax.ShapeDtypeStruct((n, w_), dt) for w_, dt in outs]
        + [jax.ShapeDtypeStruct((n_batch, SUBLANES, D_RNN), F32)] * 2,
        scratch_shapes=[pltpu.VMEM((tm, D_RNN), F32), pltpu.VMEM((tm, D_RNN), F32),
                        pltpu.VMEM((SUBLANES, D_RNN), F32), pltpu.VMEM((SUBLANES, D_RNN), F32)],
        compiler_params=_cparams(("arbitrary",)),
        name="inproj_prompt",
    )(x, g, w, cos, sin, gq, gk, cw, cb, wax, ba, bx, lam)


def _inproj_sample(x, g, w, cos, sin, gq, gk):
    n = x.shape[0]
    tm = TOK_TILE
    row = lambda width: pl.BlockSpec((tm, width), lambda i: (i, 0))
    tab = row(LANES)
    outs = [(Q_W, BF16), (KV_W, F32), (KV_W, F32), (D_RNN, F32), (D_RNN, F32), (2 * D_MODEL, F32)]
    return pl.pallas_call(
        functools.partial(_inproj_kernel, prompt=False, tiles_per_seq=None),
        grid=(n // tm,),
        in_specs=[row(D_MODEL), _resident((1, D_MODEL)), _resident((D_MODEL, N_IN)), tab, tab,
                  _resident((1, LANES)), _resident((1, LANES))],
        out_specs=[row(w_) for w_, _ in outs],
        out_shape=[jax.ShapeDtypeStruct((n, w_), dt) for w_, dt in outs],
        compiler_params=_cparams(("parallel",)),
        name="inproj_sample",
    )(x, g, w, cos, sin, gq, gk)


def _lambda_full(lq, lam_init):
    a = jnp.sum(lq[0:1, :] * lq[1:2, :], axis=-1, keepdims=True)
    b = jnp.sum(lq[2:3, :] * lq[3:4, :], axis=-1, keepdims=True)
    return jnp.exp(a) - jnp.exp(b) + lam_init


def _subln(o, g, lam_init):
    return _rmsnorm_rows(o, g) * (1.0 - lam_init)


def _pattn_kernel(q_ref, kb_ref, vb_ref, lq_ref, sg_ref, o_ref, qs_scr, m_scr, l_scr, acc_scr,
                  *, lam_init):
    tb = ATT_BLOCK
    rows = 2 * GROUP * tb
    qi = pl.program_id(2)
    lane = lax.broadcasted_iota(jnp.int32, (tb, LANES), 1)
    low = lane < HEAD_DIM
    for c in range(2):
        for g in range(GROUP):
            qg = q_ref[:, g * LANES:(g + 1) * LANES].astype(F32)
            r0 = (c * GROUP + g) * tb
            sel = jnp.where(low, qg, 0.0) if c == 0 else jnp.where(low, 0.0, qg)
            qs_scr[r0:r0 + tb, :] = sel.astype(BF16)
    m_scr[...] = jnp.full(m_scr.shape, -jnp.inf, F32)
    l_scr[...] = jnp.zeros(l_scr.shape, F32)
    acc_scr[...] = jnp.zeros(acc_scr.shape, F32)

    def update(start, n_keys, first_visible):
        kc = kb_ref[pl.ds(start, n_keys), :]
        vc = vb_ref[pl.ds(start, n_keys), :]
        s = lax.dot_general(qs_scr[...], kc, (((1,), (1,)), ((), ())),
                            preferred_element_type=F32)
        if first_visible is not None:
            t = lax.broadcasted_iota(jnp.int32, s.shape, 0) & (tb - 1)
            col = lax.broadcasted_iota(jnp.int32, s.shape, 1)
            s = jnp.where(col <= t + first_visible, s, -jnp.inf)
        m_prev = m_scr[...]
        m_new = jnp.maximum(m_prev, jnp.max(s, axis=1, keepdims=True))
        alpha = jnp.exp(m_prev - m_new)
        p = jnp.exp(s - jnp.concatenate([m_new] * (n_keys // LANES), axis=1))
        l_scr[...] = alpha * l_scr[...] + jnp.sum(p, axis=1, keepdims=True)
        acc_scr[...] = alpha * acc_scr[...] + jnp.dot(p.astype(BF16), vc,
                                                      preferred_element_type=F32)
        m_scr[...] = m_new

    nb = KEY_BLOCKS_PER_CHUNK

    def body(j, carry):
        update(pl.multiple_of(j * nb * tb, nb * tb), nb * tb, None)
        return carry

    lax.fori_loop(0, qi // nb, body, 0)

    for r in range(nb):
        @pl.when(qi % nb == r)
        def _(r=r):
            update(pl.multiple_of((qi - r) * tb, tb), (r + 1) * tb, r * tb)

    on = acc_scr[...] / l_scr[...]
    lam = _lambda_full(lq_ref[...], lam_init)
    od = on[0:rows // 2] - lam * on[rows // 2:rows]
    res = _subln(od, sg_ref[...], lam_init).astype(BF16)
    for g in range(GROUP):
        o_ref[:, g * LANES:(g + 1) * LANES] = res[g * tb:(g + 1) * tb]


def _pattn(q, kb, vb, lq, sg, n_batch, t_pad, lam_init):
    tb = ATT_BLOCK
    nq = t_pad // tb
    n = q.shape[0]
    rows = 2 * GROUP * tb
    return pl.pallas_call(
        functools.partial(_pattn_kernel, lam_init=lam_init),
        grid=(n_batch, N_KV_HEADS, nq),
        in_specs=[pl.BlockSpec((tb, GROUP * LANES), lambda b, h, i: (b * nq + i, h)),
                  pl.BlockSpec((t_pad, LANES), lambda b, h, i: (b, h)),
                  pl.BlockSpec((t_pad, LANES), lambda b, h, i: (b, h)),
                  _resident((4, HEAD_DIM)), _resident((1, LANES))],
        out_specs=pl.BlockSpec((tb, GROUP * LANES), lambda b, h, i: (b * nq + i, h)),
        out_shape=jax.ShapeDtypeStruct((n, Q_W), BF16),
        scratch_shapes=[pltpu.VMEM((rows, LANES), BF16),
                        pltpu.VMEM((rows, LANES), F32),
                        pltpu.VMEM((rows, LANES), F32),
                        pltpu.VMEM((rows, LANES), F32)],
        compiler_params=_cparams(("parallel", "parallel", "arbitrary")),
        name="pattn",
    )(q, kb, vb, lq, sg)


_DEC_ROWS = N_KV_HEADS * 2 * GROUP * 8
_ROWS_H = _DEC_ROWS // N_KV_HEADS


def _dattn_kernel(pt_ref, qb_ref, *refs, lam_init, n_steps):
    npg = PAGES_PER_STEP
    k_refs = refs[:npg]
    v_refs = refs[npg:2 * npg]
    kn_ref, vn_ref, lq_ref, sg_ref, o_ref, m_scr, l_scr, acc_scr = refs[2 * npg:]
    pc = pl.program_id(1)

    @pl.when(pc == 0)
    def _():
        m_scr[...] = jnp.full(m_scr.shape, -jnp.inf, F32)
        l_scr[...] = jnp.zeros(l_scr.shape, F32)
        acc_scr[...] = jnp.zeros(acc_scr.shape, F32)

    qb = qb_ref[...]

    def update(pages_k, pages_v, masked):
        ss = []
        for kp in pages_k:
            s = jnp.dot(qb, kp[...].astype(BF16), preferred_element_type=F32)
            if masked:
                row = lax.broadcasted_iota(jnp.int32, s.shape, 0)
                col = lax.broadcasted_iota(jnp.int32, s.shape, 1)
                s = jnp.where(col <= (row & 7), s, -jnp.inf)
            ss.append(s)
        m_prev = m_scr[...]
        m_cur = ss[0]
        for s in ss[1:]:
            m_cur = jnp.maximum(m_cur, s)
        m_new = jnp.maximum(m_prev, jnp.max(m_cur, axis=1, keepdims=True))
        alpha = jnp.exp(m_prev - m_new)
        l_new = alpha * l_scr[...]
        pv = [None] * N_KV_HEADS
        n_keys = pages_v[0].shape[0] // N_KV_HEADS
        for s, vp in zip(ss, pages_v):
            p = jnp.exp(s - m_new)
            l_new = l_new + jnp.sum(p, axis=1, keepdims=True)
            pb = p.astype(BF16)
            for h in range(N_KV_HEADS):
                vh = vp[pl.ds(h, n_keys, stride=N_KV_HEADS), :].astype(BF16)
                d = jnp.dot(pb[h * _ROWS_H:(h + 1) * _ROWS_H], vh, preferred_element_type=F32)
                pv[h] = d if pv[h] is None else pv[h] + d
        l_scr[...] = l_new
        acc_scr[...] = alpha * acc_scr[...] + jnp.concatenate(pv, axis=0)
        m_scr[...] = m_new

    update(k_refs, v_refs, False)

    @pl.when(pc == n_steps - 1)
    def _():
        update([kn_ref], [vn_ref], True)
        on = acc_scr[...] / l_scr[...]
        lam = _lambda_full(lq_ref[...], lam_init)
        for h in range(N_KV_HEADS):
            blk = on[h * _ROWS_H:(h + 1) * _ROWS_H]
            od = blk[0:_ROWS_H // 2] - lam * blk[_ROWS_H // 2:_ROWS_H]
            o_ref[h * (_ROWS_H // 2):(h + 1) * (_ROWS_H // 2), :] = _subln(od, sg_ref[...], lam_init)


def _dattn(page_table, qbd, cache_kt, cache_v2, knew_t, vnew2, lq, sg, layer, lam_init):
    db, n_pages = page_table.shape
    npg = PAGES_PER_STEP
    n_steps = n_pages // npg
    rows, page = cache_kt.shape[2], cache_kt.shape[3]

    def page_spec(i):
        return pl.BlockSpec((None, None, rows, page),
                            lambda b, pc, pt: (layer, pt[b, pc * npg + i], 0, 0))

    per_b = lambda r, w: pl.BlockSpec((None, r, w), lambda b, pc, pt: (b, 0, 0))
    const = lambda shape: pl.BlockSpec(shape, lambda b, pc, pt: (0,) * len(shape))
    grid_spec = pltpu.PrefetchScalarGridSpec(
        num_scalar_prefetch=1,
        grid=(db, n_steps),
        in_specs=[per_b(_DEC_ROWS, rows)] + [page_spec(i) for i in range(npg)]
        + [page_spec(i) for i in range(npg)]
        + [per_b(rows, page), per_b(rows, page), const((4, HEAD_DIM)), const((1, LANES))],
        out_specs=per_b(_DEC_ROWS // 2, LANES),
        scratch_shapes=[pltpu.VMEM((_DEC_ROWS, LANES), F32), pltpu.VMEM((_DEC_ROWS, LANES), F32),
                        pltpu.VMEM((_DEC_ROWS, LANES), F32)],
    )
    return pl.pallas_call(
        functools.partial(_dattn_kernel, lam_init=lam_init, n_steps=n_steps),
        grid_spec=grid_spec,
        out_shape=jax.ShapeDtypeStruct((db, _DEC_ROWS // 2, LANES), F32),
        compiler_params=_cparams(("parallel", "arbitrary")),
        name="dattn",
    )(page_table, qbd, *([cache_kt] * npg), *([cache_v2] * npg), knew_t, vnew2, lq, sg)


def _gelu_tanh(x):
    return 0.5 * x * (1.0 + jnp.tanh(math.sqrt(2.0 / math.pi) * (x + 0.044715 * (x * x * x))))


def _softplus(x):
    return jnp.maximum(x, 0.0) + jnp.log(1.0 + jnp.exp(-jnp.abs(x)))


def _lru_block(x, shifted, gr, h_in, seg_row, seg_len, cw, cb, wax, ba, bx, lam):
    xc = cb + (((shifted[2] * cw[0:1] + shifted[1] * cw[1:2]) + shifted[0] * cw[2:3]) + x * cw[3:4])
    gates = jnp.dot(xc.astype(BF16), wax, preferred_element_type=F32)
    r = _sigmoid(gates[:, :LRU_BLOCK] + ba)
    i = _sigmoid(gates[:, LRU_BLOCK:] + bx)
    log_a = (-LRU_C * _softplus(-lam)) * r
    a = jnp.exp(log_a)
    b = jnp.sqrt(1.0 - jnp.exp(2.0 * log_a)) * (i * xc)
    d = 1
    while d < seg_len:
        if d < SUBLANES:
            keep = seg_row >= d
            a_sh = pltpu.roll(a, d, 0)
            b_sh = pltpu.roll(b, d, 0)
            b = jnp.where(keep, a * b_sh + b, b)
            a = jnp.where(keep, a * a_sh, a)
        else:
            a_sh = jnp.concatenate([jnp.ones((d, LANES), F32), a[:-d]], axis=0)
            b_sh = jnp.concatenate([jnp.zeros((d, LANES), F32), b[:-d]], axis=0)
            b = a * b_sh + b
            a = a * a_sh
        d *= 2
    h = a * h_in + b
    return h, h * _gelu_tanh(gr)


def _lru_sample_kernel(xr_ref, gr_ref, p1_ref, p2_ref, p3_ref, h0_ref, cw_ref, cb_ref, wax_ref,
                       ba_ref, bx_ref, lam_ref, y_ref, h_ref, *, seg_len):
    tm = xr_ref.shape[0]
    seg_row = lax.broadcasted_iota(jnp.int32, (tm, LANES), 0) & (seg_len - 1)
    prevs = (p1_ref, p2_ref, p3_ref)
    for n in range(N_LRU_BLOCKS):
        cs = slice(n * LANES, (n + 1) * LANES)
        x = xr_ref[:, cs]
        shifted = [jnp.where(seg_row < j, prevs[j - 1][:, cs], pltpu.roll(x, j, 0))
                   for j in range(1, CONV_W)]
        h, y = _lru_block(x, shifted, gr_ref[:, cs], h0_ref[:, cs], seg_row, seg_len,
                          cw_ref[:, cs], cb_ref[:, cs], wax_ref[n], ba_ref[:, cs], bx_ref[:, cs],
                          lam_ref[:, cs])
        y_ref[:, cs] = y.astype(BF16)
        h_ref[:, cs] = h


def _lru_sample(xr, gr, prevs, h0, cw, cb, wax, ba, bx, lam, seg_len):
    n = xr.shape[0]
    full = lambda shape: pl.BlockSpec(shape, lambda i: (0,) * len(shape))
    return pl.pallas_call(
        functools.partial(_lru_sample_kernel, seg_len=seg_len),
        grid=(1,),
        in_specs=[full((n, D_RNN))] * 6 + [full((CONV_W, D_RNN)), full((1, D_RNN)),
                                            full((N_LRU_BLOCKS, LRU_BLOCK, 2 * LRU_BLOCK)),
                                            full((1, D_RNN)), full((1, D_RNN)), full((1, D_RNN))],
        out_specs=[full((n, D_RNN)), full((n, D_RNN))],
        out_shape=[jax.ShapeDtypeStruct((n, D_RNN), BF16), jax.ShapeDtypeStruct((n, D_RNN), F32)],
        compiler_params=_cparams(("arbitrary",)),
        name="lru_sample",
    )(xr, gr, *prevs, h0, cw, cb, wax, ba, bx, lam)


def _merge_kernel(o_ref, y_ref, gt_ref, x_ref, woa_ref, wol_ref, wout_ref, out_ref):
    ya = jnp.dot(o_ref[...], woa_ref[...], preferred_element_type=F32)
    yl = jnp.dot(y_ref[...], wol_ref[...], preferred_element_type=F32)
    merged = _sigmoid(gt_ref[:, :D_MODEL]) * ya + _sigmoid(gt_ref[:, D_MODEL:]) * yl
    out_ref[...] = x_ref[...] + jnp.dot(merged.astype(BF16), wout_ref[...],
                                        preferred_element_type=F32)


def _merge(o, y, gt, x, woa, wol, wout):
    n = x.shape[0]
    tm = TOK_TILE
    row = lambda width: pl.BlockSpec((tm, width), lambda i: (i, 0))
    return pl.pallas_call(
        _merge_kernel,
        grid=(n // tm,),
        in_specs=[row(Q_W), row(D_RNN), row(2 * D_MODEL), row(D_MODEL),
                  _resident((Q_W, D_MODEL)), _resident((D_RNN, D_MODEL)),
                  _resident((D_MODEL, D_MODEL))],
        out_specs=row(D_MODEL),
        out_shape=jax.ShapeDtypeStruct((n, D_MODEL), F32),
        compiler_params=_cparams(("parallel",)),
        name="merge",
    )(o, y, gt, x, woa, wol, wout)


_FF_CHUNK = 256


def _ffn_kernel(x_ref, g_ref, wgu_ref, wd_ref, out_ref, act_scr):
    x = x_ref[...]
    h = _rmsnorm_rows(x, g_ref[...]).astype(BF16)
    for c0 in range(0, D_FF, _FF_CHUNK):
        gate = jnp.dot(h, wgu_ref[:, c0:c0 + _FF_CHUNK], preferred_element_type=F32)
        up = jnp.dot(h, wgu_ref[:, D_FF + c0:D_FF + c0 + _FF_CHUNK], preferred_element_type=F32)
        act_scr[:, c0:c0 + _FF_CHUNK] = (gate * _sigmoid(gate) * up).astype(BF16)
    out_ref[...] = x + jnp.dot(act_scr[...], wd_ref[...], preferred_element_type=F32)


def _ffn(x, g, wgu, wd):
    n = x.shape[0]
    tm = TOK_TILE
    row = pl.BlockSpec((tm, D_MODEL), lambda i: (i, 0))
    return pl.pallas_call(
        _ffn_kernel,
        grid=(n // tm,),
        in_specs=[row, _resident((1, D_MODEL)), _resident((D_MODEL, 2 * D_FF)),
                  _resident((D_FF, D_MODEL))],
        out_specs=row,
        out_shape=jax.ShapeDtypeStruct((n, D_MODEL), F32),
        scratch_shapes=[pltpu.VMEM((tm, D_FF), BF16)],
        compiler_params=_cparams(("parallel",)),
        name="ffn",
    )(x, g, wgu, wd)


def _rope_tables(pos):
    inv_freq = 1.0 / (ROPE_THETA ** (jnp.arange(0, HEAD_DIM, 2, dtype=F32) / HEAD_DIM))
    ang = pos.astype(F32)[:, None] * inv_freq[None, :]
    cos = jnp.cos(ang)
    sin = jnp.sin(ang)
    return jnp.tile(cos, (1, 4)), jnp.tile(jnp.concatenate([-sin, sin], axis=-1), (1, 2))


def kernel(x_prompt, x_sample, cache_k, cache_v, state_h, state_conv, page_table, meta_tokens,
           norm1_g, w_in, q_norm_g, k_norm_g, lambda_qk, subln_g, w_o_attn,
           conv_w, conv_b, lru_w_a, lru_b_a, lru_w_x, lru_b_x, lru_lambda, w_o_lru,
           w_out, norm2_g, w_gu, w_down):
    n_batch, seq, _ = x_prompt.shape
    db, dec_seq, _ = x_sample.shape
    depth = w_in.shape[0]
    n_pages = page_table.shape[1]
    page = cache_k.shape[2]
    past_len = n_pages * page
    t_real = seq + N_META
    t_pad = -(-t_real // ATT_BLOCK) * ATT_BLOCK
    assert ATT_BLOCK == TOK_TILE and dec_seq == SUBLANES and (db * dec_seq) % TOK_TILE == 0
    assert page == LANES and n_pages % PAGES_PER_STEP == 0

    xp = jnp.pad(x_prompt, ((0, 0), (N_META, t_pad - t_real), (0, 0)))
    xp = lax.dynamic_update_slice(
        xp, jnp.broadcast_to(meta_tokens.astype(F32)[None], (n_batch, N_META, D_MODEL)), (0, 0, 0))
    xp = xp.reshape(n_batch * t_pad, D_MODEL)
    xs = x_sample.reshape(db * dec_seq, D_MODEL)

    cos_p, sin_p = _rope_tables(jnp.arange(t_pad, dtype=jnp.int32))
    pos_s = past_len + (jnp.arange(db * dec_seq, dtype=jnp.int32) % dec_seq)
    cos_s, sin_s = _rope_tables(pos_s)

    ckt = jnp.transpose(cache_k, (0, 1, 3, 4, 5, 2)).reshape(cache_k.shape[0], cache_k.shape[1], KV_W, page)
    cv2 = cache_v.reshape(cache_v.shape[0], cache_v.shape[1], page * N_KV_HEADS, 2 * HEAD_DIM)
    eye_h = jnp.eye(N_KV_HEADS, dtype=BF16)
    eye_c = jnp.eye(2, dtype=BF16)

    outs = {k: [] for k in ("kp", "vp", "hp", "cp", "ks", "vs", "hs", "cs")}
    for l in range(depth):
        lam_init = 0.8 - 0.6 * math.exp(-0.3 * l)
        g1 = norm1_g[l][None]
        win = w_in[l].astype(BF16)
        gq = jnp.tile(q_norm_g[l], 2)[None]
        gk = jnp.tile(k_norm_g[l], 2)[None]
        lq = lambda_qk[l]
        sg = subln_g[l][None]
        woa = w_o_attn[l].astype(BF16)
        cw = conv_w[l]
        cb = conv_b[l][None]
        wax = jnp.concatenate([lru_w_a[l], lru_w_x[l]], axis=-1).astype(BF16)
        ba = lru_b_a[l][None]
        bx = lru_b_x[l][None]
        lam_l = lru_lambda[l][None]
        wol = w_o_lru[l].astype(BF16)
        wout = w_out[l].astype(BF16)
        g2 = norm2_g[l][None]
        wgu = w_gu[l].astype(BF16)
        wd = w_down[l].astype(BF16)

        q, k, kb, v, vb, gt, y, hl, cvn = _inproj_prompt(xp, g1, win, cos_p, sin_p, gq, gk, cw, cb, wax, ba, bx,
                                                         lam_l, n_batch, t_pad, t_real)
        o = _pattn(q, kb, vb, lq, sg, n_batch, t_pad, lam_init)
        xp = _ffn(_merge(o, y, gt, xp, woa, wol, wout), g2, wgu, wd)
        outs["kp"].append(k.reshape(n_batch, t_pad, N_KV_HEADS, 2, HEAD_DIM)[:, :t_real])
        outs["vp"].append(v.reshape(n_batch, t_pad, N_KV_HEADS, 2 * HEAD_DIM)[:, :t_real])
        outs["hp"].append(hl[:, SUBLANES - 1])
        outs["cp"].append(cvn[:, SUBLANES - (CONV_W - 1):])

        q, k, v, xr, gr, gt = _inproj_sample(xs, g1, win, cos_s, sin_s, gq, gk)
        q6 = q.reshape(db, dec_seq, N_KV_HEADS, GROUP, 2, HEAD_DIM)
        qr = jnp.transpose(q6, (0, 2, 4, 3, 1, 5)).reshape(db, N_KV_HEADS, 2, GROUP * dec_seq, HEAD_DIM)
        qbd = jnp.einsum("bhcrd,hx,cy->bhcrxyd", qr, eye_h, eye_c).reshape(db, _DEC_ROWS, KV_W)
        knew_t = jnp.pad(jnp.transpose(k.reshape(db, dec_seq, KV_W), (0, 2, 1)),
                         ((0, 0), (0, 0), (0, page - dec_seq)))
        vnew2 = jnp.pad(v.reshape(db, dec_seq * N_KV_HEADS, 2 * HEAD_DIM),
                        ((0, 0), (0, (page - dec_seq) * N_KV_HEADS), (0, 0)))
        od = _dattn(page_table, qbd, ckt, cv2, knew_t, vnew2, lq, sg, l, lam_init)
        o = jnp.transpose(od.reshape(db, N_KV_HEADS, GROUP, dec_seq, 2 * HEAD_DIM),
                          (0, 3, 1, 2, 4)).reshape(db * dec_seq, Q_W).astype(BF16)
        conv_ext = jnp.concatenate([state_conv[l], jnp.zeros((db, dec_seq, D_RNN), F32)], axis=1)
        prevs = [conv_ext[:, CONV_W - 1 - j:CONV_W - 1 - j + dec_seq].reshape(db * dec_seq, D_RNN)
                 for j in range(1, CONV_W)]
        h0 = jnp.repeat(state_h[l], dec_seq, axis=0)
        y, hs = _lru_sample(xr, gr, prevs, h0, cw, cb, wax, ba, bx, lam_l, dec_seq)
        xs = _ffn(_merge(o, y, gt, xs, woa, wol, wout), g2, wgu, wd)
        outs["ks"].append(k.reshape(db, dec_seq, N_KV_HEADS, 2, HEAD_DIM))
        outs["vs"].append(v.reshape(db, dec_seq, N_KV_HEADS, 2 * HEAD_DIM))
        outs["hs"].append(hs.reshape(db, dec_seq, D_RNN)[:, -1])
        outs["cs"].append(xr.reshape(db, dec_seq, D_RNN)[:, dec_seq - (CONV_W - 1):])

    y_prompt = xp.reshape(n_batch, t_pad, D_MODEL)[:, N_META:t_real]
    y_sample = xs.reshape(db, dec_seq, D_MODEL)
    return (y_prompt, y_sample,
            jnp.stack(outs["kp"]), jnp.stack(outs["vp"]), jnp.stack(outs["hp"]), jnp.stack(outs["cp"]),
            jnp.stack(outs["ks"]), jnp.stack(outs["vs"]), jnp.stack(outs["hs"]), jnp.stack(outs["cs"]))
```

```python
import functools
import math

import jax
import jax.numpy as jnp
from jax import lax
from jax.experimental import pallas as pl
from jax.experimental.pallas import tpu as pltpu

F32 = jnp.float32
BF16 = jnp.bfloat16

D_MODEL = 1024
N_META = 16
N_KV_HEADS = 4
GROUP = 2
HEAD_DIM = 64
Q_W = 1024
KV_W = 512
D_RNN = 1280
N_LRU_BLOCKS = 10
LRU_BLOCK = 128
CONV_W = 4
LRU_C = 8.0
D_FF = 2816
N_IN = Q_W + 2 * KV_W + 2 * D_RNN + 2 * D_MODEL
ROPE_THETA = 10000.0
NORM_EPS = 1e-6

LANES = 128
SUBLANES = 8
TOK_TILE = 256
ATT_BLOCK = 256
PAGES_PER_STEP = 8
LAST_ROW_IN_TILE = 15
KEY_BLOCKS_PER_CHUNK = 4
VMEM_LIMIT = 48 * 1024 * 1024


def _cparams(sem):
    return pltpu.CompilerParams(dimension_semantics=sem, vmem_limit_bytes=VMEM_LIMIT)


def _resident(shape):
    return pl.BlockSpec(shape, lambda *_: (0,) * len(shape), pipeline_mode=pl.Buffered(1))


def _rmsnorm_rows(x, g):
    ms = jnp.mean(x * x, axis=-1, keepdims=True)
    return x * lax.rsqrt(ms + NORM_EPS) * g


def _sigmoid(x):
    return 1.0 / (1.0 + jnp.exp(-x))


_V_START = Q_W + KV_W
_REST_SPLITS = ((Q_W + 2 * KV_W, D_RNN), (Q_W + 2 * KV_W + D_RNN, D_RNN),
                (Q_W + 2 * KV_W + 2 * D_RNN, 2 * D_MODEL))
_N_CHUNK = 512


def _norm_rope(x, cos, sin, g, low, first_half):
    ss = x * x
    s_low = jnp.sum(jnp.where(low, ss, 0.0), axis=-1, keepdims=True)
    s_high = jnp.sum(jnp.where(low, 0.0, ss), axis=-1, keepdims=True)
    ms = jnp.where(low, s_low, s_high) * (1.0 / HEAD_DIM)
    y = x * lax.rsqrt(ms + NORM_EPS) * g
    partner = jnp.where(first_half, pltpu.roll(y, LANES - HEAD_DIM // 2, 1),
                        pltpu.roll(y, HEAD_DIM // 2, 1))
    return y * cos + partner * sin


def _inproj_kernel(x_ref, g_ref, w_ref, cos_ref, sin_ref, gq_ref, gk_ref, *refs, prompt, tiles_per_seq):
    if prompt:
        (cw_ref, cb_ref, wax_ref, ba_ref, bx_ref, lam_ref,
         q_ref, k_ref, kb_ref, v_ref, vb_ref, gt_ref, y_ref, hl_ref, cv_ref,
         xr_ref, gr_ref, xcar, hcar) = refs
    else:
        q_ref, k_ref, v_ref, xr_ref, gr_ref, gt_ref = refs
        kb_ref = vb_ref = None
    tm = x_ref.shape[0]
    h = _rmsnorm_rows(x_ref[...], g_ref[...]).astype(BF16)

    def project(ref, start, width, c0):
        cw = min(_N_CHUNK, width - c0)
        ref[:, c0:c0 + cw] = jnp.dot(h, w_ref[:, start + c0:start + c0 + cw],
                                     preferred_element_type=F32)

    for ref, (start, width) in zip((xr_ref, gr_ref), _REST_SPLITS[:2]):
        for c0 in range(0, width, _N_CHUNK):
            project(ref, start, width, c0)

    lane = lax.broadcasted_iota(jnp.int32, (tm, LANES), 1)
    low = lane < HEAD_DIM
    first_half = (lane & (HEAD_DIM - 1)) < HEAD_DIM // 2
    n_q = Q_W // LANES

    def qk_chunk(c0):
        z = jnp.dot(h, w_ref[:, c0:c0 + _N_CHUNK], preferred_element_type=F32)
        for jb in range(_N_CHUNK // LANES):
            j = c0 // LANES + jb
            out = _norm_rope(z[:, jb * LANES:(jb + 1) * LANES], cos_ref[...], sin_ref[...],
                             gq_ref[...] if j < n_q else gk_ref[...], low, first_half)
            if j < n_q:
                q_ref[:, j * LANES:(j + 1) * LANES] = (out * (HEAD_DIM ** -0.5)).astype(BF16)
            else:
                cs = slice((j - n_q) * LANES, (j - n_q + 1) * LANES)
                k_ref[:, cs] = out
                if kb_ref is not None:
                    kb_ref[:, cs] = out.astype(BF16)

    def v_chunk():
        z = jnp.dot(h, w_ref[:, _V_START:_V_START + KV_W], preferred_element_type=F32)
        v_ref[...] = z
        if vb_ref is not None:
            vb_ref[...] = z.astype(BF16)

    gt_start, gt_width = _REST_SPLITS[2]
    work = [functools.partial(qk_chunk, c0) for c0 in range(0, Q_W + KV_W, _N_CHUNK)] + [v_chunk]
    work += [functools.partial(project, gt_ref, gt_start, gt_width, c0)
             for c0 in range(0, gt_width, _N_CHUNK)]
    if not prompt:
        for item in work:
            item()
        return

    first_tile = pl.program_id(0) % tiles_per_seq == 0
    row = lax.broadcasted_iota(jnp.int32, (tm, LANES), 0)
    row8 = lax.broadcasted_iota(jnp.int32, (SUBLANES, LANES), 0)
    for n in range(N_LRU_BLOCKS):
        cs = slice(n * LANES, (n + 1) * LANES)
        x = xr_ref[:, cs]
        car = jnp.where(first_tile, 0.0, xcar[:, cs])
        h_in = jnp.where(first_tile, 0.0, hcar[0:1, cs])
        shifted = []
        for j in range(1, CONV_W):
            main = pltpu.roll(x, j, 0)
            head = jnp.where(row8 < j, pltpu.roll(car, j, 0), main[0:SUBLANES])
            shifted.append(jnp.concatenate([head, main[SUBLANES:]], axis=0))
        hs, y = _lru_block(x, shifted, gr_ref[:, cs], h_in, row, tm,
                           cw_ref[:, cs], cb_ref[:, cs], wax_ref[n], ba_ref[:, cs], bx_ref[:, cs],
                           lam_ref[:, cs])
        y_ref[:, cs] = y.astype(BF16)
        xcar[:, cs] = x[tm - SUBLANES:tm]
        hcar[0:1, cs] = hs[tm - 1:tm]
        r0 = LAST_ROW_IN_TILE - (SUBLANES - 1)
        hl_ref[:, cs] = hs[r0:r0 + SUBLANES]
        cv_ref[:, cs] = x[r0:r0 + SUBLANES]
        if n < len(work):
            work[n]()
    for item in work[N_LRU_BLOCKS:]:
        item()


def _inproj_prompt(x, g, w, cos, sin, gq, gk, cw, cb, wax, ba, bx, lam, n_batch, t_pad, t_real):
    n = x.shape[0]
    tm = TOK_TILE
    nt = t_pad // tm
    last_tile, last_row = divmod(t_real - 1, tm)
    assert last_tile == nt - 1 and last_row == LAST_ROW_IN_TILE
    row = lambda width: pl.BlockSpec((tm, width), lambda i: (i, 0))
    tab = pl.BlockSpec((tm, LANES), lambda i: (i % nt, 0))
    tail = pl.BlockSpec((None, SUBLANES, D_RNN), lambda i: (i // nt, 0, 0))
    outs = [(Q_W, BF16), (KV_W, F32), (KV_W, BF16), (KV_W, F32), (KV_W, BF16), (2 * D_MODEL, F32),
            (D_RNN, BF16)]
    return pl.pallas_call(
        functools.partial(_inproj_kernel, prompt=True, tiles_per_seq=nt),
        grid=(n // tm,),
        in_specs=[row(D_MODEL), _resident((1, D_MODEL)), _resident((D_MODEL, N_IN)), tab, tab,
                  _resident((1, LANES)), _resident((1, LANES)),
                  _resident((CONV_W, D_RNN)), _resident((1, D_RNN)),
                  _resident((N_LRU_BLOCKS, LRU_BLOCK, 2 * LRU_BLOCK)),
                  _resident((1, D_RNN)), _resident((1, D_RNN)), _resident((1, D_RNN))],
        out_specs=[row(w_) for w_, _ in outs] + [tail, tail],
        out_shape=[jax.ShapeDtypeStruct((n, w_), dt) for w_, dt in outs]
        + [jax.ShapeDtypeStruct((n_batch, SUBLANES, D_RNN), F32)] * 2,
        scratch_shapes=[pltpu.VMEM((tm, D_RNN), F32), pltpu.VMEM((tm, D_RNN), F32),
                        pltpu.VMEM((SUBLANES, D_RNN), F32), pltpu.VMEM((SUBLANES, D_RNN), F32)],
        compiler_params=_cparams(("arbitrary",)),
        name="inproj_prompt",
    )(x, g, w, cos, sin, gq, gk, cw, cb, wax, ba, bx, lam)


def _inproj_sample(x, g, w, cos, sin, gq, gk):
    n = x.shape[0]
    tm = TOK_TILE
    row = lambda width: pl.BlockSpec((tm, width), lambda i: (i, 0))
    tab = row(LANES)
    outs = [(Q_W, BF16), (KV_W, F32), (KV_W, F32), (D_RNN, F32), (D_RNN, F32), (2 * D_MODEL, F32)]
    return pl.pallas_call(
        functools.partial(_inproj_kernel, prompt=False, tiles_per_seq=None),
        grid=(n // tm,),
        in_specs=[row(D_MODEL), _resident((1, D_MODEL)), _resident((D_MODEL, N_IN)), tab, tab,
                  _resident((1, LANES)), _resident((1, LANES))],
        out_specs=[row(w_) for w_, _ in outs],
        out_shape=[jax.ShapeDtypeStruct((n, w_), dt) for w_, dt in outs],
        compiler_params=_cparams(("parallel",)),
        name="inproj_sample",
    )(x, g, w, cos, sin, gq, gk)


def _lambda_full(lq, lam_init):
    a = jnp.sum(lq[0:1, :] * lq[1:2, :], axis=-1, keepdims=True)
    b = jnp.sum(lq[2:3, :] * lq[3:4, :], axis=-1, keepdims=True)
    return jnp.exp(a) - jnp.exp(b) + lam_init


def _subln(o, g, lam_init):
    return _rmsnorm_rows(o, g) * (1.0 - lam_init)


def _pattn_kernel(q_ref, kb_ref, vb_ref, lq_ref, sg_ref, o_ref, qs_scr, m_scr, l_scr, acc_scr,
                  *, lam_init, tq, q_block):
    tb = ATT_BLOCK
    rows = 2 * GROUP * tq
    qi = pl.program_id(2) if q_block is None else q_block
    lane = lax.broadcasted_iota(jnp.int32, (tq, LANES), 1)
    low = lane < HEAD_DIM
    for c in range(2):
        for g in range(GROUP):
            qg = q_ref[:, g * LANES:(g + 1) * LANES].astype(F32)
            r0 = (c * GROUP + g) * tq
            sel = jnp.where(low, qg, 0.0) if c == 0 else jnp.where(low, 0.0, qg)
            qs_scr[r0:r0 + tq, :] = sel.astype(BF16)
    m_scr[...] = jnp.full(m_scr.shape, -jnp.inf, F32)
    l_scr[...] = jnp.zeros(l_scr.shape, F32)
    acc_scr[...] = jnp.zeros(acc_scr.shape, F32)

    def update(start, n_keys, first_visible):
        kc = kb_ref[pl.ds(start, n_keys), :]
        vc = vb_ref[pl.ds(start, n_keys), :]
        s = lax.dot_general(qs_scr[...], kc, (((1,), (1,)), ((), ())),
                            preferred_element_type=F32)
        if first_visible is not None:
            t = lax.broadcasted_iota(jnp.int32, s.shape, 0) & (tq - 1)
            col = lax.broadcasted_iota(jnp.int32, s.shape, 1)
            s = jnp.where(col <= t + first_visible, s, -jnp.inf)
        m_prev = m_scr[...]
        m_new = jnp.maximum(m_prev, jnp.max(s, axis=1, keepdims=True))
        alpha = jnp.exp(m_prev - m_new)
        p = jnp.exp(s - jnp.concatenate([m_new] * (n_keys // LANES), axis=1))
        l_scr[...] = alpha * l_scr[...] + jnp.sum(p, axis=1, keepdims=True)
        acc_scr[...] = alpha * acc_scr[...] + jnp.dot(p.astype(BF16), vc,
                                                      preferred_element_type=F32)
        m_scr[...] = m_new

    nb = KEY_BLOCKS_PER_CHUNK

    def body(j, carry):
        update(pl.multiple_of(j * nb * tb, nb * tb), nb * tb, None)
        return carry

    lax.fori_loop(0, qi // nb, body, 0)

    if q_block is None:
        for r in range(nb):
            @pl.when(qi % nb == r)
            def _(r=r):
                update(pl.multiple_of((qi - r) * tb, tb), (r + 1) * tb, r * tb)
    else:
        r = q_block % nb
        update((q_block - r) * tb, (r + 1) * tb, r * tb)

    on = acc_scr[...] / l_scr[...]
    lam = _lambda_full(lq_ref[...], lam_init)
    od = on[0:rows // 2] - lam * on[rows // 2:rows]
    res = _subln(od, sg_ref[...], lam_init).astype(BF16)
    if tq < o_ref.shape[0]:
        o_ref[...] = jnp.zeros(o_ref.shape, BF16)
    for g in range(GROUP):
        o_ref[0:tq, g * LANES:(g + 1) * LANES] = res[g * tq:(g + 1) * tq]


def _pattn(q, kb, vb, lq, sg, n_batch, t_pad, t_real, lam_init):
    tb = ATT_BLOCK
    nq = t_pad // tb
    n_last = t_real - (nq - 1) * tb
    assert 0 < n_last <= tb and n_last % (2 * SUBLANES) == 0
    kv = pl.BlockSpec((t_pad, LANES), lambda b, h, *i: (b, h))

    def call(tq, q_block, grid, q_map, o_map, o_rows, name):
        rows = 2 * GROUP * tq
        return pl.pallas_call(
            functools.partial(_pattn_kernel, lam_init=lam_init, tq=tq, q_block=q_block),
            grid=grid,
            in_specs=[pl.BlockSpec((tq, GROUP * LANES), q_map), kv, kv,
                      _resident((4, HEAD_DIM)), _resident((1, LANES))],
            out_specs=pl.BlockSpec((tb, GROUP * LANES), o_map),
            out_shape=jax.ShapeDtypeStruct((o_rows, Q_W), BF16),
            scratch_shapes=[pltpu.VMEM((rows, LANES), BF16),
                            pltpu.VMEM((rows, LANES), F32),
                            pltpu.VMEM((rows, LANES), F32),
                            pltpu.VMEM((rows, LANES), F32)],
            compiler_params=_cparams(("parallel",) * (len(grid) - 1) + ("arbitrary",)),
            name=name,
        )(q, kb, vb, lq, sg)

    o_main = call(tb, None, (n_batch, N_KV_HEADS, nq - 1),
                  lambda b, h, i: (b * nq + i, h), lambda b, h, i: (b * (nq - 1) + i, h),
                  n_batch * (nq - 1) * tb, "pattn")
    per_blk = tb // n_last
    o_last = call(n_last, nq - 1, (n_batch, N_KV_HEADS),
                  lambda b, h: ((b * nq + nq - 1) * per_blk, h), lambda b, h: (b, h),
                  n_batch * tb, "pattn_last")
    return o_main, o_last


_DEC_ROWS = N_KV_HEADS * 2 * GROUP * 8
_ROWS_H = _DEC_ROWS // N_KV_HEADS


def _dattn_kernel(pt_ref, qb_ref, *refs, lam_init, n_steps):
    npg = PAGES_PER_STEP
    k_refs = refs[:npg]
    v_refs = refs[npg:2 * npg]
    kn_ref, vn_ref, lq_ref, sg_ref, o_ref, m_scr, l_scr, acc_scr = refs[2 * npg:]
    pc = pl.program_id(1)

    @pl.when(pc == 0)
    def _():
        m_scr[...] = jnp.full(m_scr.shape, -jnp.inf, F32)
        l_scr[...] = jnp.zeros(l_scr.shape, F32)
        acc_scr[...] = jnp.zeros(acc_scr.shape, F32)

    qb = qb_ref[...]

    def update(pages_k, pages_v, masked):
        ss = []
        for kp in pages_k:
            s = jnp.dot(qb, kp[...].astype(BF16), preferred_element_type=F32)
            if masked:
                row = lax.broadcasted_iota(jnp.int32, s.shape, 0)
                col = lax.broadcasted_iota(jnp.int32, s.shape, 1)
                s = jnp.where(col <= (row & 7), s, -jnp.inf)
            ss.append(s)
        m_prev = m_scr[...]
        m_cur = ss[0]
        for s in ss[1:]:
            m_cur = jnp.maximum(m_cur, s)
        m_new = jnp.maximum(m_prev, jnp.max(m_cur, axis=1, keepdims=True))
        alpha = jnp.exp(m_prev - m_new)
        l_new = alpha * l_scr[...]
        pv = [None] * N_KV_HEADS
        n_keys = pages_v[0].shape[0] // N_KV_HEADS
        for s, vp in zip(ss, pages_v):
            p = jnp.exp(s - m_new)
            l_new = l_new + jnp.sum(p, axis=1, keepdims=True)
            pb = p.astype(BF16)
            for h in range(N_KV_HEADS):
                vh = vp[pl.ds(h, n_keys, stride=N_KV_HEADS), :].astype(BF16)
                d = jnp.dot(pb[h * _ROWS_H:(h + 1) * _ROWS_H], vh, preferred_element_type=F32)
                pv[h] = d if pv[h] is None else pv[h] + d
        l_scr[...] = l_new
        acc_scr[...] = alpha * acc_scr[...] + jnp.concatenate(pv, axis=0)
        m_scr[...] = m_new

    update(k_refs, v_refs, False)

    @pl.when(pc == n_steps - 1)
    def _():
        update([kn_ref], [vn_ref], True)
        on = acc_scr[...] / l_scr[...]
        lam = _lambda_full(lq_ref[...], lam_init)
        for h in range(N_KV_HEADS):
            blk = on[h * _ROWS_H:(h + 1) * _ROWS_H]
            od = blk[0:_ROWS_H // 2] - lam * blk[_ROWS_H // 2:_ROWS_H]
            o_ref[h * (_ROWS_H // 2):(h + 1) * (_ROWS_H // 2), :] = _subln(od, sg_ref[...], lam_init)


def _dattn(page_table, qbd, cache_kt, cache_v2, knew_t, vnew2, lq, sg, layer, lam_init):
    db, n_pages = page_table.shape
    npg = PAGES_PER_STEP
    n_steps = n_pages // npg
    rows, page = cache_kt.shape[2], cache_kt.shape[3]

    def page_spec(i):
        return pl.BlockSpec((None, None, rows, page),
                            lambda b, pc, pt: (layer, pt[b, pc * npg + i], 0, 0))

    per_b = lambda r, w: pl.BlockSpec((None, r, w), lambda b, pc, pt: (b, 0, 0))
    const = lambda shape: pl.BlockSpec(shape, lambda b, pc, pt: (0,) * len(shape))
    grid_spec = pltpu.PrefetchScalarGridSpec(
        num_scalar_prefetch=1,
        grid=(db, n_steps),
        in_specs=[per_b(_DEC_ROWS, rows)] + [page_spec(i) for i in range(npg)]
        + [page_spec(i) for i in range(npg)]
        + [per_b(rows, page), per_b(rows, page), const((4, HEAD_DIM)), const((1, LANES))],
        out_specs=per_b(_DEC_ROWS // 2, LANES),
        scratch_shapes=[pltpu.VMEM((_DEC_ROWS, LANES), F32), pltpu.VMEM((_DEC_ROWS, LANES), F32),
                        pltpu.VMEM((_DEC_ROWS, LANES), F32)],
    )
    return pl.pallas_call(
        functools.partial(_dattn_kernel, lam_init=lam_init, n_steps=n_steps),
        grid_spec=grid_spec,
        out_shape=jax.ShapeDtypeStruct((db, _DEC_ROWS // 2, LANES), F32),
        compiler_params=_cparams(("parallel", "arbitrary")),
        name="dattn",
    )(page_table, qbd, *([cache_kt] * npg), *([cache_v2] * npg), knew_t, vnew2, lq, sg)


def _gelu_tanh(x):
    return 0.5 * x * (1.0 + jnp.tanh(math.sqrt(2.0 / math.pi) * (x + 0.044715 * (x * x * x))))


def _softplus(x):
    return jnp.maximum(x, 0.0) + jnp.log(1.0 + jnp.exp(-jnp.abs(x)))


def _lru_block(x, shifted, gr, h_in, seg_row, seg_len, cw, cb, wax, ba, bx, lam):
    xc = cb + (((shifted[2] * cw[0:1] + shifted[1] * cw[1:2]) + shifted[0] * cw[2:3]) + x * cw[3:4])
    gates = jnp.dot(xc.astype(BF16), wax, preferred_element_type=F32)
    r = _sigmoid(gates[:, :LRU_BLOCK] + ba)
    i = _sigmoid(gates[:, LRU_BLOCK:] + bx)
    log_a = (-LRU_C * _softplus(-lam)) * r
    a = jnp.exp(log_a)
    b = jnp.sqrt(1.0 - jnp.exp(2.0 * log_a)) * (i * xc)
    d = 1
    while d < seg_len:
        if d < SUBLANES:
            keep = seg_row >= d
            a_sh = pltpu.roll(a, d, 0)
            b_sh = pltpu.roll(b, d, 0)
            b = jnp.where(keep, a * b_sh + b, b)
            a = jnp.where(keep, a * a_sh, a)
        else:
            a_sh = jnp.concatenate([jnp.ones((d, LANES), F32), a[:-d]], axis=0)
            b_sh = jnp.concatenate([jnp.zeros((d, LANES), F32), b[:-d]], axis=0)
            b = a * b_sh + b
            a = a * a_sh
        d *= 2
    h = a * h_in + b
    return h, h * _gelu_tanh(gr)


def _lru_sample_kernel(xr_ref, gr_ref, p1_ref, p2_ref, p3_ref, h0_ref, cw_ref, cb_ref, wax_ref,
                       ba_ref, bx_ref, lam_ref, y_ref, h_ref, *, seg_len):
    tm = xr_ref.shape[0]
    seg_row = lax.broadcasted_iota(jnp.int32, (tm, LANES), 0) & (seg_len - 1)
    prevs = (p1_ref, p2_ref, p3_ref)
    for n in range(N_LRU_BLOCKS):
        cs = slice(n * LANES, (n + 1) * LANES)
        x = xr_ref[:, cs]
        shifted = [jnp.where(seg_row < j, prevs[j - 1][:, cs], pltpu.roll(x, j, 0))
                   for j in range(1, CONV_W)]
        h, y = _lru_block(x, shifted, gr_ref[:, cs], h0_ref[:, cs], seg_row, seg_len,
                          cw_ref[:, cs], cb_ref[:, cs], wax_ref[n], ba_ref[:, cs], bx_ref[:, cs],
                          lam_ref[:, cs])
        y_ref[:, cs] = y.astype(BF16)
        h_ref[:, cs] = h


def _lru_sample(xr, gr, prevs, h0, cw, cb, wax, ba, bx, lam, seg_len):
    n = xr.shape[0]
    full = lambda shape: pl.BlockSpec(shape, lambda i: (0,) * len(shape))
    return pl.pallas_call(
        functools.partial(_lru_sample_kernel, seg_len=seg_len),
        grid=(1,),
        in_specs=[full((n, D_RNN))] * 6 + [full((CONV_W, D_RNN)), full((1, D_RNN)),
                                            full((N_LRU_BLOCKS, LRU_BLOCK, 2 * LRU_BLOCK)),
                                            full((1, D_RNN)), full((1, D_RNN)), full((1, D_RNN))],
        out_specs=[full((n, D_RNN)), full((n, D_RNN))],
        out_shape=[jax.ShapeDtypeStruct((n, D_RNN), BF16), jax.ShapeDtypeStruct((n, D_RNN), F32)],
        compiler_params=_cparams(("arbitrary",)),
        name="lru_sample",
    )(xr, gr, *prevs, h0, cw, cb, wax, ba, bx, lam)


def _merge_kernel(o_ref, *refs, tiles_per_seq):
    if tiles_per_seq is None:
        y_ref, gt_ref, x_ref, woa_ref, wol_ref, wout_ref, out_ref = refs
        o = o_ref[...]
    else:
        ol_ref, y_ref, gt_ref, x_ref, woa_ref, wol_ref, wout_ref, out_ref = refs
        is_last = pl.program_id(0) % tiles_per_seq == tiles_per_seq - 1
        o = jnp.where(is_last, ol_ref[...], o_ref[...])
    ya = jnp.dot(o, woa_ref[...], preferred_element_type=F32)
    yl = jnp.dot(y_ref[...], wol_ref[...], preferred_element_type=F32)
    merged = _sigmoid(gt_ref[:, :D_MODEL]) * ya + _sigmoid(gt_ref[:, D_MODEL:]) * yl
    out_ref[...] = x_ref[...] + jnp.dot(merged.astype(BF16), wout_ref[...],
                                        preferred_element_type=F32)


def _merge(o, y, gt, x, woa, wol, wout, tiles_per_seq=None):
    n = x.shape[0]
    tm = TOK_TILE
    row = lambda width: pl.BlockSpec((tm, width), lambda i: (i, 0))
    if tiles_per_seq is None:
        o_specs, o_args = [row(Q_W)], [o]
    else:
        nt = tiles_per_seq
        o_specs = [pl.BlockSpec((tm, Q_W),
                                lambda i: ((i // nt) * (nt - 1) + jnp.minimum(i % nt, nt - 2), 0)),
                   pl.BlockSpec((tm, Q_W), lambda i: (i // nt, 0))]
        o_args = list(o)
    return pl.pallas_call(
        functools.partial(_merge_kernel, tiles_per_seq=tiles_per_seq),
        grid=(n // tm,),
        in_specs=o_specs + [row(D_RNN), row(2 * D_MODEL), row(D_MODEL),
                            _resident((Q_W, D_MODEL)), _resident((D_RNN, D_MODEL)),
                            _resident((D_MODEL, D_MODEL))],
        out_specs=row(D_MODEL),
        out_shape=jax.ShapeDtypeStruct((n, D_MODEL), F32),
        compiler_params=_cparams(("parallel",)),
        name="merge",
    )(*o_args, y, gt, x, woa, wol, wout)


_FF_CHUNK = 256


def _ffn_kernel(x_ref, g_ref, wgu_ref, wd_ref, out_ref, act_scr):
    x = x_ref[...]
    h = _rmsnorm_rows(x, g_ref[...]).astype(BF16)
    for c0 in range(0, D_FF, _FF_CHUNK):
        gate = jnp.dot(h, wgu_ref[:, c0:c0 + _FF_CHUNK], preferred_element_type=F32)
        up = jnp.dot(h, wgu_ref[:, D_FF + c0:D_FF + c0 + _FF_CHUNK], preferred_element_type=F32)
        act_scr[:, c0:c0 + _FF_CHUNK] = (gate * _sigmoid(gate) * up).astype(BF16)
    out_ref[...] = x + jnp.dot(act_scr[...], wd_ref[...], preferred_element_type=F32)


def _ffn(x, g, wgu, wd):
    n = x.shape[0]
    tm = TOK_TILE
    row = pl.BlockSpec((tm, D_MODEL), lambda i: (i, 0))
    return pl.pallas_call(
        _ffn_kernel,
        grid=(n // tm,),
        in_specs=[row, _resident((1, D_MODEL)), _resident((D_MODEL, 2 * D_FF)),
                  _resident((D_FF, D_MODEL))],
        out_specs=row,
        out_shape=jax.ShapeDtypeStruct((n, D_MODEL), F32),
        scratch_shapes=[pltpu.VMEM((tm, D_FF), BF16)],
        compiler_params=_cparams(("parallel",)),
        name="ffn",
    )(x, g, wgu, wd)


def _rope_tables(pos):
    inv_freq = 1.0 / (ROPE_THETA ** (jnp.arange(0, HEAD_DIM, 2, dtype=F32) / HEAD_DIM))
    ang = pos.astype(F32)[:, None] * inv_freq[None, :]
    cos = jnp.cos(ang)
    sin = jnp.sin(ang)
    return jnp.tile(cos, (1, 4)), jnp.tile(jnp.concatenate([-sin, sin], axis=-1), (1, 2))


def kernel(x_prompt, x_sample, cache_k, cache_v, state_h, state_conv, page_table, meta_tokens,
           norm1_g, w_in, q_norm_g, k_norm_g, lambda_qk, subln_g, w_o_attn,
           conv_w, conv_b, lru_w_a, lru_b_a, lru_w_x, lru_b_x, lru_lambda, w_o_lru,
           w_out, norm2_g, w_gu, w_down):
    n_batch, seq, _ = x_prompt.shape
    db, dec_seq, _ = x_sample.shape
    depth = w_in.shape[0]
    n_pages = page_table.shape[1]
    page = cache_k.shape[2]
    past_len = n_pages * page
    t_real = seq + N_META
    t_pad = -(-t_real // ATT_BLOCK) * ATT_BLOCK
    assert ATT_BLOCK == TOK_TILE and dec_seq == SUBLANES and (db * dec_seq) % TOK_TILE == 0
    assert page == LANES and n_pages % PAGES_PER_STEP == 0

    xp = jnp.pad(x_prompt, ((0, 0), (N_META, t_pad - t_real), (0, 0)))
    xp = lax.dynamic_update_slice(
        xp, jnp.broadcast_to(meta_tokens.astype(F32)[None], (n_batch, N_META, D_MODEL)), (0, 0, 0))
    xp = xp.reshape(n_batch * t_pad, D_MODEL)
    xs = x_sample.reshape(db * dec_seq, D_MODEL)

    cos_p, sin_p = _rope_tables(jnp.arange(t_pad, dtype=jnp.int32))
    pos_s = past_len + (jnp.arange(db * dec_seq, dtype=jnp.int32) % dec_seq)
    cos_s, sin_s = _rope_tables(pos_s)

    ckt = jnp.transpose(cache_k, (0, 1, 3, 4, 5, 2)).reshape(cache_k.shape[0], cache_k.shape[1], KV_W, page)
    cv2 = cache_v.reshape(cache_v.shape[0], cache_v.shape[1], page * N_KV_HEADS, 2 * HEAD_DIM)
    eye_h = jnp.eye(N_KV_HEADS, dtype=BF16)
    eye_c = jnp.eye(2, dtype=BF16)

    outs = {k: [] for k in ("kp", "vp", "hp", "cp", "ks", "vs", "hs", "cs")}
    for l in range(depth):
        lam_init = 0.8 - 0.6 * math.exp(-0.3 * l)
        g1 = norm1_g[l][None]
        win = w_in[l].astype(BF16)
        gq = jnp.tile(q_norm_g[l], 2)[None]
        gk = jnp.tile(k_norm_g[l], 2)[None]
        lq = lambda_qk[l]
        sg = subln_g[l][None]
        woa = w_o_attn[l].astype(BF16)
        cw = conv_w[l]
        cb = conv_b[l][None]
        wax = jnp.concatenate([lru_w_a[l], lru_w_x[l]], axis=-1).astype(BF16)
        ba = lru_b_a[l][None]
        bx = lru_b_x[l][None]
        lam_l = lru_lambda[l][None]
        wol = w_o_lru[l].astype(BF16)
        wout = w_out[l].astype(BF16)
        g2 = norm2_g[l][None]
        wgu = w_gu[l].astype(BF16)
        wd = w_down[l].astype(BF16)

        q, k, kb, v, vb, gt, y, hl, cvn = _inproj_prompt(xp, g1, win, cos_p, sin_p, gq, gk, cw, cb, wax, ba, bx,
                                                         lam_l, n_batch, t_pad, t_real)
        o = _pattn(q, kb, vb, lq, sg, n_batch, t_pad, t_real, lam_init)
        xp = _ffn(_merge(o, y, gt, xp, woa, wol, wout, t_pad // TOK_TILE), g2, wgu, wd)
        outs["kp"].append(k.reshape(n_batch, t_pad, N_KV_HEADS, 2, HEAD_DIM)[:, :t_real])
        outs["vp"].append(v.reshape(n_batch, t_pad, N_KV_HEADS, 2 * HEAD_DIM)[:, :t_real])
        outs["hp"].append(hl[:, SUBLANES - 1])
        outs["cp"].append(cvn[:, SUBLANES - (CONV_W - 1):])

        q, k, v, xr, gr, gt = _inproj_sample(xs, g1, win, cos_s, sin_s, gq, gk)
        q6 = q.reshape(db, dec_seq, N_KV_HEADS, GROUP, 2, HEAD_DIM)
        qr = jnp.transpose(q6, (0, 2, 4, 3, 1, 5)).reshape(db, N_KV_HEADS, 2, GROUP * dec_seq, HEAD_DIM)
        qbd = jnp.einsum("bhcrd,hx,cy->bhcrxyd", qr, eye_h, eye_c).reshape(db, _DEC_ROWS, KV_W)
        knew_t = jnp.pad(jnp.transpose(k.reshape(db, dec_seq, KV_W), (0, 2, 1)),
                         ((0, 0), (0, 0), (0, page - dec_seq)))
        vnew2 = jnp.pad(v.reshape(db, dec_seq * N_KV_HEADS, 2 * HEAD_DIM),
                        ((0, 0), (0, (page - dec_seq) * N_KV_HEADS), (0, 0)))
        od = _dattn(page_table, qbd, ckt, cv2, knew_t, vnew2, lq, sg, l, lam_init)
        o = jnp.transpose(od.reshape(db, N_KV_HEADS, GROUP, dec_seq, 2 * HEAD_DIM),
                          (0, 3, 1, 2, 4)).reshape(db * dec_seq, Q_W).astype(BF16)
        conv_ext = jnp.concatenate([state_conv[l], jnp.zeros((db, dec_seq, D_RNN), F32)], axis=1)
        prevs = [conv_ext[:, CONV_W - 1 - j:CONV_W - 1 - j + dec_seq].reshape(db * dec_seq, D_RNN)
                 for j in range(1, CONV_W)]
        h0 = jnp.repeat(state_h[l], dec_seq, axis=0)
        y, hs = _lru_sample(xr, gr, prevs, h0, cw, cb, wax, ba, bx, lam_l, dec_seq)
        xs = _ffn(_merge(o, y, gt, xs, woa, wol, wout), g2, wgu, wd)
        outs["ks"].append(k.reshape(db, dec_seq, N_KV_HEADS, 2, HEAD_DIM))
        outs["vs"].append(v.reshape(db, dec_seq, N_KV_HEADS, 2 * HEAD_DIM))
        outs["hs"].append(hs.reshape(db, dec_seq, D_RNN)[:, -1])
        outs["cs"].append(xr.reshape(db, dec_seq, D_RNN)[:, dec_seq - (CONV_W - 1):])

    y_prompt = xp.reshape(n_batch, t_pad, D_MODEL)[:, N_META:t_real]
    y_sample = xs.reshape(db, dec_seq, D_MODEL)
    return (y_prompt, y_sample,
            jnp.stack(outs["kp"]), jnp.stack(outs["vp"]), jnp.stack(outs["hp"]), jnp.stack(outs["cp"]),
            jnp.stack(outs["ks"]), jnp.stack(outs["vs"]), jnp.stack(outs["hs"]), jnp.stack(outs["cs"]))
```

```python
import functools
import math

import jax
import jax.numpy as jnp
from jax import lax
from jax.experimental import pallas as pl
from jax.experimental.pallas import tpu as pltpu

F32 = jnp.float32
BF16 = jnp.bfloat16

D_MODEL = 1024
N_META = 16
N_KV_HEADS = 4
GROUP = 2
HEAD_DIM = 64
Q_W = 1024
KV_W = 512
D_RNN = 1280
N_LRU_BLOCKS = 10
LRU_BLOCK = 128
CONV_W = 4
LRU_C = 8.0
D_FF = 2816
N_IN = Q_W + 2 * KV_W + 2 * D_RNN + 2 * D_MODEL
ROPE_THETA = 10000.0
NORM_EPS = 1e-6

LANES = 128
SUBLANES = 8
TOK_TILE = 256
ATT_BLOCK = 256
PAGES_PER_STEP = 8
LAST_ROW_IN_TILE = 15
KEY_BLOCKS_PER_CHUNK = 4
VMEM_LIMIT = 48 * 1024 * 1024


def _cparams(sem):
    return pltpu.CompilerParams(dimension_semantics=sem, vmem_limit_bytes=VMEM_LIMIT)


def _resident(shape):
    return pl.BlockSpec(shape, lambda *_: (0,) * len(shape), pipeline_mode=pl.Buffered(1))


def _rmsnorm_rows(x, g):
    ms = jnp.mean(x * x, axis=-1, keepdims=True)
    return x * lax.rsqrt(ms + NORM_EPS) * g


def _sigmoid(x):
    return 1.0 / (1.0 + jnp.exp(-x))


_V_START = Q_W + KV_W
_REST_SPLITS = ((Q_W + 2 * KV_W, D_RNN), (Q_W + 2 * KV_W + D_RNN, D_RNN),
                (Q_W + 2 * KV_W + 2 * D_RNN, 2 * D_MODEL))
_N_CHUNK = 512


def _norm_rope(x, cos, sin, g, low, first_half):
    ss = x * x
    s_low = jnp.sum(jnp.where(low, ss, 0.0), axis=-1, keepdims=True)
    s_high = jnp.sum(jnp.where(low, 0.0, ss), axis=-1, keepdims=True)
    ms = jnp.where(low, s_low, s_high) * (1.0 / HEAD_DIM)
    y = x * lax.rsqrt(ms + NORM_EPS) * g
    partner = jnp.where(first_half, pltpu.roll(y, LANES - HEAD_DIM // 2, 1),
                        pltpu.roll(y, HEAD_DIM // 2, 1))
    return y * cos + partner * sin


def _inproj_kernel(x_ref, g_ref, w_ref, cos_ref, sin_ref, gq_ref, gk_ref, *refs, prompt, tiles_per_seq):
    if prompt:
        (cw_ref, cb_ref, wax_ref, ba_ref, bx_ref, lam_ref,
         q_ref, k_ref, kb_ref, v_ref, vb_ref, gt_ref, y_ref, hl_ref, cv_ref,
         xr_ref, gr_ref, xcar, hcar) = refs
    else:
        q_ref, k_ref, v_ref, xr_ref, gr_ref, gt_ref = refs
        kb_ref = vb_ref = None
    tm = x_ref.shape[0]
    h = _rmsnorm_rows(x_ref[...], g_ref[...]).astype(BF16)

    def project(ref, start, width, c0):
        cw = min(_N_CHUNK, width - c0)
        ref[:, c0:c0 + cw] = jnp.dot(h, w_ref[:, start + c0:start + c0 + cw],
                                     preferred_element_type=F32)

    for ref, (start, width) in zip((xr_ref, gr_ref), _REST_SPLITS[:2]):
        for c0 in range(0, width, _N_CHUNK):
            project(ref, start, width, c0)

    lane = lax.broadcasted_iota(jnp.int32, (tm, LANES), 1)
    low = lane < HEAD_DIM
    first_half = (lane & (HEAD_DIM - 1)) < HEAD_DIM // 2
    n_q = Q_W // LANES

    def qk_chunk(c0):
        z = jnp.dot(h, w_ref[:, c0:c0 + _N_CHUNK], preferred_element_type=F32)
        for jb in range(_N_CHUNK // LANES):
            j = c0 // LANES + jb
            out = _norm_rope(z[:, jb * LANES:(jb + 1) * LANES], cos_ref[...], sin_ref[...],
                             gq_ref[...] if j < n_q else gk_ref[...], low, first_half)
            if j < n_q:
                q_ref[:, j * LANES:(j + 1) * LANES] = (out * (HEAD_DIM ** -0.5)).astype(BF16)
            else:
                cs = slice((j - n_q) * LANES, (j - n_q + 1) * LANES)
                if prompt:
                    k_ref[cs, :] = out.T
                    kb_ref[:, cs] = out.astype(BF16)
                else:
                    k_ref[:, cs] = out

    def v_chunk():
        z = jnp.dot(h, w_ref[:, _V_START:_V_START + KV_W], preferred_element_type=F32)
        if prompt:
            for hh in range(N_KV_HEADS):
                v_ref[pl.ds(hh, tm, stride=N_KV_HEADS), :] = z[:, hh * LANES:(hh + 1) * LANES]
            vb_ref[...] = z.astype(BF16)
        else:
            v_ref[...] = z

    gt_start, gt_width = _REST_SPLITS[2]
    work = [functools.partial(qk_chunk, c0) for c0 in range(0, Q_W + KV_W, _N_CHUNK)] + [v_chunk]
    work += [functools.partial(project, gt_ref, gt_start, gt_width, c0)
             for c0 in range(0, gt_width, _N_CHUNK)]
    if not prompt:
        for item in work:
            item()
        return

    first_tile = pl.program_id(0) % tiles_per_seq == 0
    row = lax.broadcasted_iota(jnp.int32, (tm, LANES), 0)
    row8 = lax.broadcasted_iota(jnp.int32, (SUBLANES, LANES), 0)
    for n in range(N_LRU_BLOCKS):
        cs = slice(n * LANES, (n + 1) * LANES)
        x = xr_ref[:, cs]
        car = jnp.where(first_tile, 0.0, xcar[:, cs])
        h_in = jnp.where(first_tile, 0.0, hcar[0:1, cs])
        shifted = []
        for j in range(1, CONV_W):
            main = pltpu.roll(x, j, 0)
            head = jnp.where(row8 < j, pltpu.roll(car, j, 0), main[0:SUBLANES])
            shifted.append(jnp.concatenate([head, main[SUBLANES:]], axis=0))
        hs, y = _lru_block(x, shifted, gr_ref[:, cs], h_in, row, tm,
                           cw_ref[:, cs], cb_ref[:, cs], wax_ref[n], ba_ref[:, cs], bx_ref[:, cs],
                           lam_ref[:, cs])
        y_ref[:, cs] = y.astype(BF16)
        xcar[:, cs] = x[tm - SUBLANES:tm]
        hcar[0:1, cs] = hs[tm - 1:tm]
        r0 = LAST_ROW_IN_TILE - (SUBLANES - 1)
        hl_ref[:, cs] = hs[r0:r0 + SUBLANES]
        cv_ref[:, cs] = x[r0:r0 + SUBLANES]
        if n < len(work):
            work[n]()
    for item in work[N_LRU_BLOCKS:]:
        item()


def _inproj_prompt(x, g, w, cos, sin, gq, gk, cw, cb, wax, ba, bx, lam, n_batch, t_pad, t_real):
    n = x.shape[0]
    tm = TOK_TILE
    nt = t_pad // tm
    last_tile, last_row = divmod(t_real - 1, tm)
    assert last_tile == nt - 1 and last_row == LAST_ROW_IN_TILE
    row = lambda width: pl.BlockSpec((tm, width), lambda i: (i, 0))
    tab = pl.BlockSpec((tm, LANES), lambda i: (i % nt, 0))
    tail = pl.BlockSpec((None, SUBLANES, D_RNN), lambda i: (i // nt, 0, 0))
    outs = [(Q_W, BF16), None, (KV_W, BF16), None, (KV_W, BF16), (2 * D_MODEL, F32), (D_RNN, BF16)]
    specs = [row(w_[0]) if w_ else None for w_ in outs]
    shapes = [jax.ShapeDtypeStruct((n, w_[0]), w_[1]) if w_ else None for w_ in outs]
    specs[1] = pl.BlockSpec((None, KV_W, tm), lambda i: (i // nt, 0, i % nt))
    shapes[1] = jax.ShapeDtypeStruct((n_batch, KV_W, t_real), F32)
    specs[3] = pl.BlockSpec((None, tm * N_KV_HEADS, LANES), lambda i: (i // nt, i % nt, 0))
    shapes[3] = jax.ShapeDtypeStruct((n_batch, t_real * N_KV_HEADS, LANES), F32)
    return pl.pallas_call(
        functools.partial(_inproj_kernel, prompt=True, tiles_per_seq=nt),
        grid=(n // tm,),
        in_specs=[row(D_MODEL), _resident((1, D_MODEL)), _resident((D_MODEL, N_IN)), tab, tab,
                  _resident((1, LANES)), _resident((1, LANES)),
                  _resident((CONV_W, D_RNN)), _resident((1, D_RNN)),
                  _resident((N_LRU_BLOCKS, LRU_BLOCK, 2 * LRU_BLOCK)),
                  _resident((1, D_RNN)), _resident((1, D_RNN)), _resident((1, D_RNN))],
        out_specs=specs + [tail, tail],
        out_shape=shapes + [jax.ShapeDtypeStruct((n_batch, SUBLANES, D_RNN), F32)] * 2,
        scratch_shapes=[pltpu.VMEM((tm, D_RNN), F32), pltpu.VMEM((tm, D_RNN), F32),
                        pltpu.VMEM((SUBLANES, D_RNN), F32), pltpu.VMEM((SUBLANES, D_RNN), F32)],
        compiler_params=_cparams(("arbitrary",)),
        name="inproj_prompt",
    )(x, g, w, cos, sin, gq, gk, cw, cb, wax, ba, bx, lam)


def _inproj_sample(x, g, w, cos, sin, gq, gk):
    n = x.shape[0]
    tm = TOK_TILE
    row = lambda width: pl.BlockSpec((tm, width), lambda i: (i, 0))
    tab = row(LANES)
    outs = [(Q_W, BF16), (KV_W, F32), (KV_W, F32), (D_RNN, F32), (D_RNN, F32), (2 * D_MODEL, F32)]
    return pl.pallas_call(
        functools.partial(_inproj_kernel, prompt=False, tiles_per_seq=None),
        grid=(n // tm,),
        in_specs=[row(D_MODEL), _resident((1, D_MODEL)), _resident((D_MODEL, N_IN)), tab, tab,
                  _resident((1, LANES)), _resident((1, LANES))],
        out_specs=[row(w_) for w_, _ in outs],
        out_shape=[jax.ShapeDtypeStruct((n, w_), dt) for w_, dt in outs],
        compiler_params=_cparams(("parallel",)),
        name="inproj_sample",
    )(x, g, w, cos, sin, gq, gk)


def _lambda_full(lq, lam_init):
    a = jnp.sum(lq[0:1, :] * lq[1:2, :], axis=-1, keepdims=True)
    b = jnp.sum(lq[2:3, :] * lq[3:4, :], axis=-1, keepdims=True)
    return jnp.exp(a) - jnp.exp(b) + lam_init


def _subln(o, g, lam_init):
    return _rmsnorm_rows(o, g) * (1.0 - lam_init)


def _pattn_kernel(q_ref, kb_ref, vb_ref, lq_ref, sg_ref, o_ref, qs_scr, m_scr, l_scr, acc_scr,
                  *, lam_init, tq, q_block):
    tb = ATT_BLOCK
    rows = 2 * GROUP * tq
    qi = pl.program_id(2) if q_block is None else q_block
    lane = lax.broadcasted_iota(jnp.int32, (tq, LANES), 1)
    low = lane < HEAD_DIM
    for c in range(2):
        for g in range(GROUP):
            qg = q_ref[:, g * LANES:(g + 1) * LANES].astype(F32)
            r0 = (c * GROUP + g) * tq
            sel = jnp.where(low, qg, 0.0) if c == 0 else jnp.where(low, 0.0, qg)
            qs_scr[r0:r0 + tq, :] = sel.astype(BF16)
    m_scr[...] = jnp.full(m_scr.shape, -jnp.inf, F32)
    l_scr[...] = jnp.zeros(l_scr.shape, F32)
    acc_scr[...] = jnp.zeros(acc_scr.shape, F32)

    def update(start, n_keys, first_visible):
        kc = kb_ref[pl.ds(start, n_keys), :]
        vc = vb_ref[pl.ds(start, n_keys), :]
        s = lax.dot_general(qs_scr[...], kc, (((1,), (1,)), ((), ())),
                            preferred_element_type=F32)
        if first_visible is not None:
            t = lax.broadcasted_iota(jnp.int32, s.shape, 0) & (tq - 1)
            col = lax.broadcasted_iota(jnp.int32, s.shape, 1)
            s = jnp.where(col <= t + first_visible, s, -jnp.inf)
        m_prev = m_scr[...]
        m_new = jnp.maximum(m_prev, jnp.max(s, axis=1, keepdims=True))
        alpha = jnp.exp(m_prev - m_new)
        p = jnp.exp(s - jnp.concatenate([m_new] * (n_keys // LANES), axis=1))
        l_scr[...] = alpha * l_scr[...] + jnp.sum(p, axis=1, keepdims=True)
        acc_scr[...] = alpha * acc_scr[...] + jnp.dot(p.astype(BF16), vc,
                                                      preferred_element_type=F32)
        m_scr[...] = m_new

    nb = KEY_BLOCKS_PER_CHUNK

    def body(j, carry):
        update(pl.multiple_of(j * nb * tb, nb * tb), nb * tb, None)
        return carry

    lax.fori_loop(0, qi // nb, body, 0)

    if q_block is None:
        for r in range(nb):
            @pl.when(qi % nb == r)
            def _(r=r):
                update(pl.multiple_of((qi - r) * tb, tb), (r + 1) * tb, r * tb)
    else:
        r = q_block % nb
        update((q_block - r) * tb, (r + 1) * tb, r * tb)

    on = acc_scr[...] / l_scr[...]
    lam = _lambda_full(lq_ref[...], lam_init)
    od = on[0:rows // 2] - lam * on[rows // 2:rows]
    res = _subln(od, sg_ref[...], lam_init).astype(BF16)
    if tq < o_ref.shape[0]:
        o_ref[...] = jnp.zeros(o_ref.shape, BF16)
    for g in range(GROUP):
        o_ref[0:tq, g * LANES:(g + 1) * LANES] = res[g * tq:(g + 1) * tq]


def _pattn(q, kb, vb, lq, sg, n_batch, t_pad, t_real, lam_init):
    tb = ATT_BLOCK
    nq = t_pad // tb
    n_last = t_real - (nq - 1) * tb
    assert 0 < n_last <= tb and n_last % (2 * SUBLANES) == 0
    kv = pl.BlockSpec((t_pad, LANES), lambda b, h, *i: (b, h))

    def call(tq, q_block, grid, q_map, o_map, o_rows, name):
        rows = 2 * GROUP * tq
        return pl.pallas_call(
            functools.partial(_pattn_kernel, lam_init=lam_init, tq=tq, q_block=q_block),
            grid=grid,
            in_specs=[pl.BlockSpec((tq, GROUP * LANES), q_map), kv, kv,
                      _resident((4, HEAD_DIM)), _resident((1, LANES))],
            out_specs=pl.BlockSpec((tb, GROUP * LANES), o_map),
            out_shape=jax.ShapeDtypeStruct((o_rows, Q_W), BF16),
            scratch_shapes=[pltpu.VMEM((rows, LANES), BF16),
                            pltpu.VMEM((rows, LANES), F32),
                            pltpu.VMEM((rows, LANES), F32),
                            pltpu.VMEM((rows, LANES), F32)],
            compiler_params=_cparams(("parallel",) * (len(grid) - 1) + ("arbitrary",)),
            name=name,
        )(q, kb, vb, lq, sg)

    o_main = call(tb, None, (n_batch, N_KV_HEADS, nq - 1),
                  lambda b, h, i: (b * nq + i, h), lambda b, h, i: (b * (nq - 1) + i, h),
                  n_batch * (nq - 1) * tb, "pattn")
    per_blk = tb // n_last
    o_last = call(n_last, nq - 1, (n_batch, N_KV_HEADS),
                  lambda b, h: ((b * nq + nq - 1) * per_blk, h), lambda b, h: (b, h),
                  n_batch * tb, "pattn_last")
    return o_main, o_last


_DEC_ROWS = N_KV_HEADS * 2 * GROUP * 8
_ROWS_H = _DEC_ROWS // N_KV_HEADS


def _dattn_kernel(pt_ref, qb_ref, *refs, lam_init, n_steps):
    npg = PAGES_PER_STEP
    k_refs = refs[:npg]
    v_refs = refs[npg:2 * npg]
    kn_ref, vn_ref, lq_ref, sg_ref, o_ref, m_scr, l_scr, acc_scr = refs[2 * npg:]
    pc = pl.program_id(1)

    @pl.when(pc == 0)
    def _():
        m_scr[...] = jnp.full(m_scr.shape, -jnp.inf, F32)
        l_scr[...] = jnp.zeros(l_scr.shape, F32)
        acc_scr[...] = jnp.zeros(acc_scr.shape, F32)

    qb = qb_ref[...]

    def update(pages_k, pages_v, masked):
        ss = []
        for kp in pages_k:
            s = jnp.dot(qb, kp[...].astype(BF16), preferred_element_type=F32)
            if masked:
                row = lax.broadcasted_iota(jnp.int32, s.shape, 0)
                col = lax.broadcasted_iota(jnp.int32, s.shape, 1)
                s = jnp.where(col <= (row & 7), s, -jnp.inf)
            ss.append(s)
        m_prev = m_scr[...]
        m_cur = ss[0]
        for s in ss[1:]:
            m_cur = jnp.maximum(m_cur, s)
        m_new = jnp.maximum(m_prev, jnp.max(m_cur, axis=1, keepdims=True))
        alpha = jnp.exp(m_prev - m_new)
        l_new = alpha * l_scr[...]
        pv = [None] * N_KV_HEADS
        n_keys = pages_v[0].shape[0] // N_KV_HEADS
        for s, vp in zip(ss, pages_v):
            p = jnp.exp(s - m_new)
            l_new = l_new + jnp.sum(p, axis=1, keepdims=True)
            pb = p.astype(BF16)
            for h in range(N_KV_HEADS):
                vh = vp[pl.ds(h, n_keys, stride=N_KV_HEADS), :].astype(BF16)
                d = jnp.dot(pb[h * _ROWS_H:(h + 1) * _ROWS_H], vh, preferred_element_type=F32)
                pv[h] = d if pv[h] is None else pv[h] + d
        l_scr[...] = l_new
        acc_scr[...] = alpha * acc_scr[...] + jnp.concatenate(pv, axis=0)
        m_scr[...] = m_new

    update(k_refs, v_refs, False)

    @pl.when(pc == n_steps - 1)
    def _():
        update([kn_ref], [vn_ref], True)
        on = acc_scr[...] / l_scr[...]
        lam = _lambda_full(lq_ref[...], lam_init)
        for h in range(N_KV_HEADS):
            blk = on[h * _ROWS_H:(h + 1) * _ROWS_H]
            od = blk[0:_ROWS_H // 2] - lam * blk[_ROWS_H // 2:_ROWS_H]
            o_ref[h * (_ROWS_H // 2):(h + 1) * (_ROWS_H // 2), :] = _subln(od, sg_ref[...], lam_init)


def _dattn(page_table, qbd, cache_kt, cache_v2, knew_t, vnew2, lq, sg, layer, lam_init):
    db, n_pages = page_table.shape
    npg = PAGES_PER_STEP
    n_steps = n_pages // npg
    rows, page = cache_kt.shape[2], cache_kt.shape[3]

    def page_spec(i):
        return pl.BlockSpec((None, None, rows, page),
                            lambda b, pc, pt: (layer, pt[b, pc * npg + i], 0, 0))

    per_b = lambda r, w: pl.BlockSpec((None, r, w), lambda b, pc, pt: (b, 0, 0))
    const = lambda shape: pl.BlockSpec(shape, lambda b, pc, pt: (0,) * len(shape))
    grid_spec = pltpu.PrefetchScalarGridSpec(
        num_scalar_prefetch=1,
        grid=(db, n_steps),
        in_specs=[per_b(_DEC_ROWS, rows)] + [page_spec(i) for i in range(npg)]
        + [page_spec(i) for i in range(npg)]
        + [per_b(rows, page), per_b(rows, page), const((4, HEAD_DIM)), const((1, LANES))],
        out_specs=per_b(_DEC_ROWS // 2, LANES),
        scratch_shapes=[pltpu.VMEM((_DEC_ROWS, LANES), F32), pltpu.VMEM((_DEC_ROWS, LANES), F32),
                        pltpu.VMEM((_DEC_ROWS, LANES), F32)],
    )
    return pl.pallas_call(
        functools.partial(_dattn_kernel, lam_init=lam_init, n_steps=n_steps),
        grid_spec=grid_spec,
        out_shape=jax.ShapeDtypeStruct((db, _DEC_ROWS // 2, LANES), F32),
        compiler_params=_cparams(("parallel", "arbitrary")),
        name="dattn",
    )(page_table, qbd, *([cache_kt] * npg), *([cache_v2] * npg), knew_t, vnew2, lq, sg)


def _gelu_tanh(x):
    return 0.5 * x * (1.0 + jnp.tanh(math.sqrt(2.0 / math.pi) * (x + 0.044715 * (x * x * x))))


def _softplus(x):
    return jnp.maximum(x, 0.0) + jnp.log(1.0 + jnp.exp(-jnp.abs(x)))


def _lru_block(x, shifted, gr, h_in, seg_row, seg_len, cw, cb, wax, ba, bx, lam):
    xc = cb + (((shifted[2] * cw[0:1] + shifted[1] * cw[1:2]) + shifted[0] * cw[2:3]) + x * cw[3:4])
    gates = jnp.dot(xc.astype(BF16), wax, preferred_element_type=F32)
    r = _sigmoid(gates[:, :LRU_BLOCK] + ba)
    i = _sigmoid(gates[:, LRU_BLOCK:] + bx)
    log_a = (-LRU_C * _softplus(-lam)) * r
    a = jnp.exp(log_a)
    b = jnp.sqrt(1.0 - jnp.exp(2.0 * log_a)) * (i * xc)
    d = 1
    while d < seg_len:
        if d < SUBLANES:
            keep = seg_row >= d
            a_sh = pltpu.roll(a, d, 0)
            b_sh = pltpu.roll(b, d, 0)
            b = jnp.where(keep, a * b_sh + b, b)
            a = jnp.where(keep, a * a_sh, a)
        else:
            a_sh = jnp.concatenate([jnp.ones((d, LANES), F32), a[:-d]], axis=0)
            b_sh = jnp.concatenate([jnp.zeros((d, LANES), F32), b[:-d]], axis=0)
            b = a * b_sh + b
            a = a * a_sh
        d *= 2
    h = a * h_in + b
    return h, h * _gelu_tanh(gr)


def _lru_sample_kernel(xr_ref, gr_ref, p1_ref, p2_ref, p3_ref, h0_ref, cw_ref, cb_ref, wax_ref,
                       ba_ref, bx_ref, lam_ref, y_ref, h_ref, *, seg_len):
    tm = xr_ref.shape[0]
    seg_row = lax.broadcasted_iota(jnp.int32, (tm, LANES), 0) & (seg_len - 1)
    prevs = (p1_ref, p2_ref, p3_ref)
    for n in range(N_LRU_BLOCKS):
        cs = slice(n * LANES, (n + 1) * LANES)
        x = xr_ref[:, cs]
        shifted = [jnp.where(seg_row < j, prevs[j - 1][:, cs], pltpu.roll(x, j, 0))
                   for j in range(1, CONV_W)]
        h, y = _lru_block(x, shifted, gr_ref[:, cs], h0_ref[:, cs], seg_row, seg_len,
                          cw_ref[:, cs], cb_ref[:, cs], wax_ref[n], ba_ref[:, cs], bx_ref[:, cs],
                          lam_ref[:, cs])
        y_ref[:, cs] = y.astype(BF16)
        h_ref[:, cs] = h


def _lru_sample(xr, gr, prevs, h0, cw, cb, wax, ba, bx, lam, seg_len):
    n = xr.shape[0]
    full = lambda shape: pl.BlockSpec(shape, lambda i: (0,) * len(shape))
    return pl.pallas_call(
        functools.partial(_lru_sample_kernel, seg_len=seg_len),
        grid=(1,),
        in_specs=[full((n, D_RNN))] * 6 + [full((CONV_W, D_RNN)), full((1, D_RNN)),
                                            full((N_LRU_BLOCKS, LRU_BLOCK, 2 * LRU_BLOCK)),
                                            full((1, D_RNN)), full((1, D_RNN)), full((1, D_RNN))],
        out_specs=[full((n, D_RNN)), full((n, D_RNN))],
        out_shape=[jax.ShapeDtypeStruct((n, D_RNN), BF16), jax.ShapeDtypeStruct((n, D_RNN), F32)],
        compiler_params=_cparams(("arbitrary",)),
        name="lru_sample",
    )(xr, gr, *prevs, h0, cw, cb, wax, ba, bx, lam)


def _merge_kernel(o_ref, *refs, tiles_per_seq):
    if tiles_per_seq is None:
        y_ref, gt_ref, x_ref, woa_ref, wol_ref, wout_ref, out_ref = refs
        o = o_ref[...]
    else:
        ol_ref, y_ref, gt_ref, x_ref, woa_ref, wol_ref, wout_ref, out_ref = refs
        is_last = pl.program_id(0) % tiles_per_seq == tiles_per_seq - 1
        o = jnp.where(is_last, ol_ref[...], o_ref[...])
    ya = jnp.dot(o, woa_ref[...], preferred_element_type=F32)
    yl = jnp.dot(y_ref[...], wol_ref[...], preferred_element_type=F32)
    merged = _sigmoid(gt_ref[:, :D_MODEL]) * ya + _sigmoid(gt_ref[:, D_MODEL:]) * yl
    out_ref[...] = x_ref[...] + jnp.dot(merged.astype(BF16), wout_ref[...],
                                        preferred_element_type=F32)


def _merge(o, y, gt, x, woa, wol, wout, tiles_per_seq=None):
    n = x.shape[0]
    tm = TOK_TILE
    row = lambda width: pl.BlockSpec((tm, width), lambda i: (i, 0))
    if tiles_per_seq is None:
        o_specs, o_args = [row(Q_W)], [o]
    else:
        nt = tiles_per_seq
        o_specs = [pl.BlockSpec((tm, Q_W),
                                lambda i: ((i // nt) * (nt - 1) + jnp.minimum(i % nt, nt - 2), 0)),
                   pl.BlockSpec((tm, Q_W), lambda i: (i // nt, 0))]
        o_args = list(o)
    return pl.pallas_call(
        functools.partial(_merge_kernel, tiles_per_seq=tiles_per_seq),
        grid=(n // tm,),
        in_specs=o_specs + [row(D_RNN), row(2 * D_MODEL), row(D_MODEL),
                            _resident((Q_W, D_MODEL)), _resident((D_RNN, D_MODEL)),
                            _resident((D_MODEL, D_MODEL))],
        out_specs=row(D_MODEL),
        out_shape=jax.ShapeDtypeStruct((n, D_MODEL), F32),
        compiler_params=_cparams(("parallel",)),
        name="merge",
    )(*o_args, y, gt, x, woa, wol, wout)


_FF_CHUNK = 256


def _ffn_kernel(x_ref, g_ref, wgu_ref, wd_ref, out_ref, act_scr):
    x = x_ref[...]
    h = _rmsnorm_rows(x, g_ref[...]).astype(BF16)
    for c0 in range(0, D_FF, _FF_CHUNK):
        gate = jnp.dot(h, wgu_ref[:, c0:c0 + _FF_CHUNK], preferred_element_type=F32)
        up = jnp.dot(h, wgu_ref[:, D_FF + c0:D_FF + c0 + _FF_CHUNK], preferred_element_type=F32)
        act_scr[:, c0:c0 + _FF_CHUNK] = (gate * _sigmoid(gate) * up).astype(BF16)
    out_ref[...] = x + jnp.dot(act_scr[...], wd_ref[...], preferred_element_type=F32)


def _ffn(x, g, wgu, wd):
    n = x.shape[0]
    tm = TOK_TILE
    row = pl.BlockSpec((tm, D_MODEL), lambda i: (i, 0))
    return pl.pallas_call(
        _ffn_kernel,
        grid=(n // tm,),
        in_specs=[row, _resident((1, D_MODEL)), _resident((D_MODEL, 2 * D_FF)),
                  _resident((D_FF, D_MODEL))],
        out_specs=row,
        out_shape=jax.ShapeDtypeStruct((n, D_MODEL), F32),
        scratch_shapes=[pltpu.VMEM((tm, D_FF), BF16)],
        compiler_params=_cparams(("parallel",)),
        name="ffn",
    )(x, g, wgu, wd)


def _rope_tables(pos):
    inv_freq = 1.0 / (ROPE_THETA ** (jnp.arange(0, HEAD_DIM, 2, dtype=F32) / HEAD_DIM))
    ang = pos.astype(F32)[:, None] * inv_freq[None, :]
    cos = jnp.cos(ang)
    sin = jnp.sin(ang)
    return jnp.tile(cos, (1, 4)), jnp.tile(jnp.concatenate([-sin, sin], axis=-1), (1, 2))


def kernel(x_prompt, x_sample, cache_k, cache_v, state_h, state_conv, page_table, meta_tokens,
           norm1_g, w_in, q_norm_g, k_norm_g, lambda_qk, subln_g, w_o_attn,
           conv_w, conv_b, lru_w_a, lru_b_a, lru_w_x, lru_b_x, lru_lambda, w_o_lru,
           w_out, norm2_g, w_gu, w_down):
    n_batch, seq, _ = x_prompt.shape
    db, dec_seq, _ = x_sample.shape
    depth = w_in.shape[0]
    n_pages = page_table.shape[1]
    page = cache_k.shape[2]
    past_len = n_pages * page
    t_real = seq + N_META
    t_pad = -(-t_real // ATT_BLOCK) * ATT_BLOCK
    assert ATT_BLOCK == TOK_TILE and dec_seq == SUBLANES and (db * dec_seq) % TOK_TILE == 0
    assert page == LANES and n_pages % PAGES_PER_STEP == 0

    xp = jnp.pad(x_prompt, ((0, 0), (N_META, t_pad - t_real), (0, 0)))
    xp = lax.dynamic_update_slice(
        xp, jnp.broadcast_to(meta_tokens.astype(F32)[None], (n_batch, N_META, D_MODEL)), (0, 0, 0))
    xp = xp.reshape(n_batch * t_pad, D_MODEL)
    xs = x_sample.reshape(db * dec_seq, D_MODEL)

    cos_p, sin_p = _rope_tables(jnp.arange(t_pad, dtype=jnp.int32))
    pos_s = past_len + (jnp.arange(db * dec_seq, dtype=jnp.int32) % dec_seq)
    cos_s, sin_s = _rope_tables(pos_s)

    ckt = jnp.transpose(cache_k, (0, 1, 3, 4, 5, 2)).reshape(cache_k.shape[0], cache_k.shape[1], KV_W, page)
    cv2 = cache_v.reshape(cache_v.shape[0], cache_v.shape[1], page * N_KV_HEADS, 2 * HEAD_DIM)
    eye_h = jnp.eye(N_KV_HEADS, dtype=BF16)
    eye_c = jnp.eye(2, dtype=BF16)

    outs = {k: [] for k in ("kp", "vp", "hp", "cp", "ks", "vs", "hs", "cs")}
    for l in range(depth):
        lam_init = 0.8 - 0.6 * math.exp(-0.3 * l)
        g1 = norm1_g[l][None]
        win = w_in[l].astype(BF16)
        gq = jnp.tile(q_norm_g[l], 2)[None]
        gk = jnp.tile(k_norm_g[l], 2)[None]
        lq = lambda_qk[l]
        sg = subln_g[l][None]
        woa = w_o_attn[l].astype(BF16)
        cw = conv_w[l]
        cb = conv_b[l][None]
        wax = jnp.concatenate([lru_w_a[l], lru_w_x[l]], axis=-1).astype(BF16)
        ba = lru_b_a[l][None]
        bx = lru_b_x[l][None]
        lam_l = lru_lambda[l][None]
        wol = w_o_lru[l].astype(BF16)
        wout = w_out[l].astype(BF16)
        g2 = norm2_g[l][None]
        wgu = w_gu[l].astype(BF16)
        wd = w_down[l].astype(BF16)

        q, k, kb, v, vb, gt, y, hl, cvn = _inproj_prompt(xp, g1, win, cos_p, sin_p, gq, gk, cw, cb, wax, ba, bx,
                                                         lam_l, n_batch, t_pad, t_real)
        o = _pattn(q, kb, vb, lq, sg, n_batch, t_pad, t_real, lam_init)
        xp = _ffn(_merge(o, y, gt, xp, woa, wol, wout, t_pad // TOK_TILE), g2, wgu, wd)
        outs["kp"].append(jnp.transpose(k.reshape(n_batch, N_KV_HEADS, 2, HEAD_DIM, t_real), (0, 4, 1, 2, 3)))
        outs["vp"].append(v.reshape(n_batch, t_real, N_KV_HEADS, 2 * HEAD_DIM))
        outs["hp"].append(hl[:, SUBLANES - 1])
        outs["cp"].append(cvn[:, SUBLANES - (CONV_W - 1):])

        q, k, v, xr, gr, gt = _inproj_sample(xs, g1, win, cos_s, sin_s, gq, gk)
        q6 = q.reshape(db, dec_seq, N_KV_HEADS, GROUP, 2, HEAD_DIM)
        qr = jnp.transpose(q6, (0, 2, 4, 3, 1, 5)).reshape(db, N_KV_HEADS, 2, GROUP * dec_seq, HEAD_DIM)
        qbd = jnp.einsum("bhcrd,hx,cy->bhcrxyd", qr, eye_h, eye_c).reshape(db, _DEC_ROWS, KV_W)
        knew_t = jnp.pad(jnp.transpose(k.reshape(db, dec_seq, KV_W), (0, 2, 1)),
                         ((0, 0), (0, 0), (0, page - dec_seq)))
        vnew2 = jnp.pad(v.reshape(db, dec_seq * N_KV_HEADS, 2 * HEAD_DIM),
                        ((0, 0), (0, (page - dec_seq) * N_KV_HEADS), (0, 0)))
        od = _dattn(page_table, qbd, ckt, cv2, knew_t, vnew2, lq, sg, l, lam_init)
        o = jnp.transpose(od.reshape(db, N_KV_HEADS, GROUP, dec_seq, 2 * HEAD_DIM),
                          (0, 3, 1, 2, 4)).reshape(db * dec_seq, Q_W).astype(BF16)
        conv_ext = jnp.concatenate([state_conv[l], jnp.zeros((db, dec_seq, D_RNN), F32)], axis=1)
        prevs = [conv_ext[:, CONV_W - 1 - j:CONV_W - 1 - j + dec_seq].reshape(db * dec_seq, D_RNN)
                 for j in range(1, CONV_W)]
        h0 = jnp.repeat(state_h[l], dec_seq, axis=0)
        y, hs = _lru_sample(xr, gr, prevs, h0, cw, cb, wax, ba, bx, lam_l, dec_seq)
        xs = _ffn(_merge(o, y, gt, xs, woa, wol, wout), g2, wgu, wd)
        outs["ks"].append(k.reshape(db, dec_seq, N_KV_HEADS, 2, HEAD_DIM))
        outs["vs"].append(v.reshape(db, dec_seq, N_KV_HEADS, 2 * HEAD_DIM))
        outs["hs"].append(hs.reshape(db, dec_seq, D_RNN)[:, -1])
        outs["cs"].append(xr.reshape(db, dec_seq, D_RNN)[:, dec_seq - (CONV_W - 1):])

    y_prompt = xp.reshape(n_batch, t_pad, D_MODEL)[:, N_META:t_real]
    y_sample = xs.reshape(db, dec_seq, D_MODEL)
    return (y_prompt, y_sample,
            jnp.stack(outs["kp"]), jnp.stack(outs["vp"]), jnp.stack(outs["hp"]), jnp.stack(outs["cp"]),
            jnp.stack(outs["ks"]), jnp.stack(outs["vs"]), jnp.stack(outs["hs"]), jnp.stack(outs["cs"]))
```

```python
import functools
import math

import jax
import jax.numpy as jnp
from jax import lax
from jax.experimental import pallas as pl
from jax.experimental.pallas import tpu as pltpu

F32 = jnp.float32
BF16 = jnp.bfloat16

D_MODEL = 1024
N_META = 16
N_KV_HEADS = 4
GROUP = 2
HEAD_DIM = 64
Q_W = 1024
KV_W = 512
D_RNN = 1280
N_LRU_BLOCKS = 10
LRU_BLOCK = 128
CONV_W = 4
LRU_C = 8.0
D_FF = 2816
N_IN = Q_W + 2 * KV_W + 2 * D_RNN + 2 * D_MODEL
ROPE_THETA = 10000.0
NORM_EPS = 1e-6

LANES = 128
SUBLANES = 8
TOK_TILE = 256
ATT_BLOCK = 256
PAGES_PER_STEP = 8
LAST_ROW_IN_TILE = 15
Q_TILE_BLOCKS = 2
KEY_BLOCKS_PER_CHUNK = 4
VMEM_LIMIT = 48 * 1024 * 1024


def _cparams(sem):
    return pltpu.CompilerParams(dimension_semantics=sem, vmem_limit_bytes=VMEM_LIMIT)


def _resident(shape):
    return pl.BlockSpec(shape, lambda *_: (0,) * len(shape), pipeline_mode=pl.Buffered(1))


def _rmsnorm_rows(x, g):
    ms = jnp.mean(x * x, axis=-1, keepdims=True)
    return x * lax.rsqrt(ms + NORM_EPS) * g


def _sigmoid(x):
    return 1.0 / (1.0 + jnp.exp(-x))


_V_START = Q_W + KV_W
_REST_SPLITS = ((Q_W + 2 * KV_W, D_RNN), (Q_W + 2 * KV_W + D_RNN, D_RNN),
                (Q_W + 2 * KV_W + 2 * D_RNN, 2 * D_MODEL))
_N_CHUNK = 512


def _norm_rope(x, cos, sin, g, low, first_half):
    ss = x * x
    s_low = jnp.sum(jnp.where(low, ss, 0.0), axis=-1, keepdims=True)
    s_high = jnp.sum(jnp.where(low, 0.0, ss), axis=-1, keepdims=True)
    ms = jnp.where(low, s_low, s_high) * (1.0 / HEAD_DIM)
    y = x * lax.rsqrt(ms + NORM_EPS) * g
    partner = jnp.where(first_half, pltpu.roll(y, LANES - HEAD_DIM // 2, 1),
                        pltpu.roll(y, HEAD_DIM // 2, 1))
    return y * cos + partner * sin


def _inproj_kernel(x_ref, g_ref, w_ref, cos_ref, sin_ref, gq_ref, gk_ref, *refs, prompt, tiles_per_seq):
    if prompt:
        (cw_ref, cb_ref, wax_ref, ba_ref, bx_ref, lam_ref,
         q_ref, k_ref, kb_ref, v_ref, vb_ref, gt_ref, y_ref, hl_ref, cv_ref,
         xr_ref, gr_ref, xcar, hcar) = refs
    else:
        q_ref, k_ref, v_ref, xr_ref, gr_ref, gt_ref = refs
        kb_ref = vb_ref = None
    tm = x_ref.shape[0]
    h = _rmsnorm_rows(x_ref[...], g_ref[...]).astype(BF16)

    def project(ref, start, width, c0):
        cw = min(_N_CHUNK, width - c0)
        ref[:, c0:c0 + cw] = jnp.dot(h, w_ref[:, start + c0:start + c0 + cw],
                                     preferred_element_type=F32)

    for ref, (start, width) in zip((xr_ref, gr_ref), _REST_SPLITS[:2]):
        for c0 in range(0, width, _N_CHUNK):
            project(ref, start, width, c0)

    lane = lax.broadcasted_iota(jnp.int32, (tm, LANES), 1)
    low = lane < HEAD_DIM
    first_half = (lane & (HEAD_DIM - 1)) < HEAD_DIM // 2
    n_q = Q_W // LANES

    def qk_chunk(c0):
        z = jnp.dot(h, w_ref[:, c0:c0 + _N_CHUNK], preferred_element_type=F32)
        for jb in range(_N_CHUNK // LANES):
            j = c0 // LANES + jb
            out = _norm_rope(z[:, jb * LANES:(jb + 1) * LANES], cos_ref[...], sin_ref[...],
                             gq_ref[...] if j < n_q else gk_ref[...], low, first_half)
            if j < n_q:
                q_ref[:, j * LANES:(j + 1) * LANES] = (out * (HEAD_DIM ** -0.5)).astype(BF16)
            else:
                cs = slice((j - n_q) * LANES, (j - n_q + 1) * LANES)
                if prompt:
                    k_ref[cs, :] = out.T
                    kb_ref[:, cs] = out.astype(BF16)
                else:
                    k_ref[:, cs] = out

    def v_chunk():
        z = jnp.dot(h, w_ref[:, _V_START:_V_START + KV_W], preferred_element_type=F32)
        if prompt:
            for hh in range(N_KV_HEADS):
                v_ref[pl.ds(hh, tm, stride=N_KV_HEADS), :] = z[:, hh * LANES:(hh + 1) * LANES]
            vb_ref[...] = z.astype(BF16)
        else:
            v_ref[...] = z

    gt_start, gt_width = _REST_SPLITS[2]
    work = [functools.partial(qk_chunk, c0) for c0 in range(0, Q_W + KV_W, _N_CHUNK)] + [v_chunk]
    work += [functools.partial(project, gt_ref, gt_start, gt_width, c0)
             for c0 in range(0, gt_width, _N_CHUNK)]
    if not prompt:
        for item in work:
            item()
        return

    first_tile = pl.program_id(0) % tiles_per_seq == 0
    row = lax.broadcasted_iota(jnp.int32, (tm, LANES), 0)
    row8 = lax.broadcasted_iota(jnp.int32, (SUBLANES, LANES), 0)
    for n in range(N_LRU_BLOCKS):
        cs = slice(n * LANES, (n + 1) * LANES)
        x = xr_ref[:, cs]
        car = jnp.where(first_tile, 0.0, xcar[:, cs])
        h_in = jnp.where(first_tile, 0.0, hcar[0:1, cs])
        shifted = []
        for j in range(1, CONV_W):
            main = pltpu.roll(x, j, 0)
            head = jnp.where(row8 < j, pltpu.roll(car, j, 0), main[0:SUBLANES])
            shifted.append(jnp.concatenate([head, main[SUBLANES:]], axis=0))
        hs, y = _lru_block(x, shifted, gr_ref[:, cs], h_in, row, tm,
                           cw_ref[:, cs], cb_ref[:, cs], wax_ref[n], ba_ref[:, cs], bx_ref[:, cs],
                           lam_ref[:, cs])
        y_ref[:, cs] = y.astype(BF16)
        xcar[:, cs] = x[tm - SUBLANES:tm]
        hcar[0:1, cs] = hs[tm - 1:tm]
        r0 = LAST_ROW_IN_TILE - (SUBLANES - 1)
        hl_ref[:, cs] = hs[r0:r0 + SUBLANES]
        cv_ref[:, cs] = x[r0:r0 + SUBLANES]
        if n < len(work):
            work[n]()
    for item in work[N_LRU_BLOCKS:]:
        item()


def _inproj_prompt(x, g, w, cos, sin, gq, gk, cw, cb, wax, ba, bx, lam, n_batch, t_pad, t_real):
    n = x.shape[0]
    tm = TOK_TILE
    nt = t_pad // tm
    last_tile, last_row = divmod(t_real - 1, tm)
    assert last_tile == nt - 1 and last_row == LAST_ROW_IN_TILE
    row = lambda width: pl.BlockSpec((tm, width), lambda i: (i, 0))
    tab = pl.BlockSpec((tm, LANES), lambda i: (i % nt, 0))
    tail = pl.BlockSpec((None, SUBLANES, D_RNN), lambda i: (i // nt, 0, 0))
    outs = [(Q_W, BF16), None, (KV_W, BF16), None, (KV_W, BF16), (2 * D_MODEL, F32), (D_RNN, BF16)]
    specs = [row(w_[0]) if w_ else None for w_ in outs]
    shapes = [jax.ShapeDtypeStruct((n, w_[0]), w_[1]) if w_ else None for w_ in outs]
    specs[1] = pl.BlockSpec((None, KV_W, tm), lambda i: (i // nt, 0, i % nt))
    shapes[1] = jax.ShapeDtypeStruct((n_batch, KV_W, t_real), F32)
    specs[3] = pl.BlockSpec((None, tm * N_KV_HEADS, LANES), lambda i: (i // nt, i % nt, 0))
    shapes[3] = jax.ShapeDtypeStruct((n_batch, t_real * N_KV_HEADS, LANES), F32)
    return pl.pallas_call(
        functools.partial(_inproj_kernel, prompt=True, tiles_per_seq=nt),
        grid=(n // tm,),
        in_specs=[row(D_MODEL), _resident((1, D_MODEL)), _resident((D_MODEL, N_IN)), tab, tab,
                  _resident((1, LANES)), _resident((1, LANES)),
                  _resident((CONV_W, D_RNN)), _resident((1, D_RNN)),
                  _resident((N_LRU_BLOCKS, LRU_BLOCK, 2 * LRU_BLOCK)),
                  _resident((1, D_RNN)), _resident((1, D_RNN)), _resident((1, D_RNN))],
        out_specs=specs + [tail, tail],
        out_shape=shapes + [jax.ShapeDtypeStruct((n_batch, SUBLANES, D_RNN), F32)] * 2,
        scratch_shapes=[pltpu.VMEM((tm, D_RNN), F32), pltpu.VMEM((tm, D_RNN), F32),
                        pltpu.VMEM((SUBLANES, D_RNN), F32), pltpu.VMEM((SUBLANES, D_RNN), F32)],
        compiler_params=_cparams(("arbitrary",)),
        name="inproj_prompt",
    )(x, g, w, cos, sin, gq, gk, cw, cb, wax, ba, bx, lam)


def _inproj_sample(x, g, w, cos, sin, gq, gk):
    n = x.shape[0]
    tm = TOK_TILE
    row = lambda width: pl.BlockSpec((tm, width), lambda i: (i, 0))
    tab = row(LANES)
    outs = [(Q_W, BF16), (KV_W, F32), (KV_W, F32), (D_RNN, F32), (D_RNN, F32), (2 * D_MODEL, F32)]
    return pl.pallas_call(
        functools.partial(_inproj_kernel, prompt=False, tiles_per_seq=None),
        grid=(n // tm,),
        in_specs=[row(D_MODEL), _resident((1, D_MODEL)), _resident((D_MODEL, N_IN)), tab, tab,
                  _resident((1, LANES)), _resident((1, LANES))],
        out_specs=[row(w_) for w_, _ in outs],
        out_shape=[jax.ShapeDtypeStruct((n, w_), dt) for w_, dt in outs],
        compiler_params=_cparams(("parallel",)),
        name="inproj_sample",
    )(x, g, w, cos, sin, gq, gk)


def _lambda_full(lq, lam_init):
    a = jnp.sum(lq[0:1, :] * lq[1:2, :], axis=-1, keepdims=True)
    b = jnp.sum(lq[2:3, :] * lq[3:4, :], axis=-1, keepdims=True)
    return jnp.exp(a) - jnp.exp(b) + lam_init


def _subln(o, g, lam_init):
    return _rmsnorm_rows(o, g) * (1.0 - lam_init)


def _pattn_kernel(q_ref, kb_ref, vb_ref, lq_ref, sg_ref, o_ref, qs_scr, m_scr, l_scr, acc_scr,
                  *, lam_init, tq, q_block):
    tb = ATT_BLOCK
    rows = 2 * GROUP * tq
    q_blocks = max(tq // tb, 1)
    qi = pl.program_id(2) * q_blocks if q_block is None else q_block
    lane = lax.broadcasted_iota(jnp.int32, (tq, LANES), 1)
    low = lane < HEAD_DIM
    for c in range(2):
        for g in range(GROUP):
            qg = q_ref[:, g * LANES:(g + 1) * LANES].astype(F32)
            r0 = (c * GROUP + g) * tq
            sel = jnp.where(low, qg, 0.0) if c == 0 else jnp.where(low, 0.0, qg)
            qs_scr[r0:r0 + tq, :] = sel.astype(BF16)
    m_scr[...] = jnp.full(m_scr.shape, -jnp.inf, F32)
    l_scr[...] = jnp.zeros(l_scr.shape, F32)
    acc_scr[...] = jnp.zeros(acc_scr.shape, F32)

    def update(start, n_keys, first_visible):
        kc = kb_ref[pl.ds(start, n_keys), :]
        vc = vb_ref[pl.ds(start, n_keys), :]
        s = lax.dot_general(qs_scr[...], kc, (((1,), (1,)), ((), ())),
                            preferred_element_type=F32)
        if first_visible is not None:
            t = lax.broadcasted_iota(jnp.int32, s.shape, 0) & (tq - 1)
            col = lax.broadcasted_iota(jnp.int32, s.shape, 1)
            s = jnp.where(col <= t + first_visible, s, -jnp.inf)
        m_prev = m_scr[...]
        m_new = jnp.maximum(m_prev, jnp.max(s, axis=1, keepdims=True))
        alpha = jnp.exp(m_prev - m_new)
        p = jnp.exp(s - jnp.concatenate([m_new] * (n_keys // LANES), axis=1))
        l_scr[...] = alpha * l_scr[...] + jnp.sum(p, axis=1, keepdims=True)
        acc_scr[...] = alpha * acc_scr[...] + jnp.dot(p.astype(BF16), vc,
                                                      preferred_element_type=F32)
        m_scr[...] = m_new

    nb = KEY_BLOCKS_PER_CHUNK

    def body(j, carry):
        update(pl.multiple_of(j * nb * tb, nb * tb), nb * tb, None)
        return carry

    lax.fori_loop(0, qi // nb, body, 0)

    if q_block is None:
        for r in range(0, nb, q_blocks):
            @pl.when(qi % nb == r)
            def _(r=r):
                update(pl.multiple_of((qi - r) * tb, tb), (r + q_blocks) * tb, r * tb)
    else:
        r = q_block % nb
        update((q_block - r) * tb, (r + q_blocks) * tb, r * tb)

    on = acc_scr[...] / l_scr[...]
    lam = _lambda_full(lq_ref[...], lam_init)
    od = on[0:rows // 2] - lam * on[rows // 2:rows]
    res = _subln(od, sg_ref[...], lam_init).astype(BF16)
    if tq < o_ref.shape[0]:
        o_ref[...] = jnp.zeros(o_ref.shape, BF16)
    for g in range(GROUP):
        o_ref[0:tq, g * LANES:(g + 1) * LANES] = res[g * tq:(g + 1) * tq]


def _pattn(q, kb, vb, lq, sg, n_batch, t_pad, t_real, lam_init):
    tb = ATT_BLOCK
    nq = t_pad // tb
    n_last = t_real - (nq - 1) * tb
    assert 0 < n_last <= tb and n_last % (2 * SUBLANES) == 0
    kv = pl.BlockSpec((t_pad, LANES), lambda b, h, *i: (b, h))

    q3 = q.reshape(n_batch, t_pad, Q_W)

    def call(tq, q_block, grid, q_map, o_map, o_rows, name):
        rows = 2 * GROUP * tq
        return pl.pallas_call(
            functools.partial(_pattn_kernel, lam_init=lam_init, tq=tq, q_block=q_block),
            grid=grid,
            in_specs=[pl.BlockSpec((None, tq, GROUP * LANES), q_map), kv, kv,
                      _resident((4, HEAD_DIM)), _resident((1, LANES))],
            out_specs=pl.BlockSpec((max(tq, tb), GROUP * LANES), o_map),
            out_shape=jax.ShapeDtypeStruct((o_rows, Q_W), BF16),
            scratch_shapes=[pltpu.VMEM((rows, LANES), BF16),
                            pltpu.VMEM((rows, LANES), F32),
                            pltpu.VMEM((rows, LANES), F32),
                            pltpu.VMEM((rows, LANES), F32)],
            compiler_params=_cparams(("parallel",) * (len(grid) - 1) + ("arbitrary",)),
            name=name,
        )(q3, kb, vb, lq, sg)

    tq = Q_TILE_BLOCKS * tb
    n_main = (nq - 1) * tb // tq
    assert n_main * tq == (nq - 1) * tb and KEY_BLOCKS_PER_CHUNK % Q_TILE_BLOCKS == 0
    o_main = call(tq, None, (n_batch, N_KV_HEADS, n_main),
                  lambda b, h, i: (b, i, h), lambda b, h, i: (b * n_main + i, h),
                  n_batch * (nq - 1) * tb, "pattn")
    o_last = call(n_last, nq - 1, (n_batch, N_KV_HEADS),
                  lambda b, h: (b, (nq - 1) * tb // n_last, h), lambda b, h: (b, h),
                  n_batch * tb, "pattn_last")
    return o_main, o_last


_DEC_ROWS = N_KV_HEADS * 2 * GROUP * 8
_ROWS_H = _DEC_ROWS // N_KV_HEADS


def _dattn_kernel(pt_ref, qb_ref, *refs, lam_init, n_steps):
    npg = PAGES_PER_STEP
    k_refs = refs[:npg]
    v_refs = refs[npg:2 * npg]
    kn_ref, vn_ref, lq_ref, sg_ref, o_ref, m_scr, l_scr, acc_scr = refs[2 * npg:]
    pc = pl.program_id(1)

    @pl.when(pc == 0)
    def _():
        m_scr[...] = jnp.full(m_scr.shape, -jnp.inf, F32)
        l_scr[...] = jnp.zeros(l_scr.shape, F32)
        acc_scr[...] = jnp.zeros(acc_scr.shape, F32)

    qb = qb_ref[...]

    def update(pages_k, pages_v, masked):
        ss = []
        for kp in pages_k:
            s = jnp.dot(qb, kp[...].astype(BF16), preferred_element_type=F32)
            if masked:
                row = lax.broadcasted_iota(jnp.int32, s.shape, 0)
                col = lax.broadcasted_iota(jnp.int32, s.shape, 1)
                s = jnp.where(col <= (row & 7), s, -jnp.inf)
            ss.append(s)
        m_prev = m_scr[...]
        m_cur = ss[0]
        for s in ss[1:]:
            m_cur = jnp.maximum(m_cur, s)
        m_new = jnp.maximum(m_prev, jnp.max(m_cur, axis=1, keepdims=True))
        alpha = jnp.exp(m_prev - m_new)
        l_new = alpha * l_scr[...]
        pv = [None] * N_KV_HEADS
        n_keys = pages_v[0].shape[0] // N_KV_HEADS
        for s, vp in zip(ss, pages_v):
            p = jnp.exp(s - m_new)
            l_new = l_new + jnp.sum(p, axis=1, keepdims=True)
            pb = p.astype(BF16)
            for h in range(N_KV_HEADS):
                vh = vp[pl.ds(h, n_keys, stride=N_KV_HEADS), :].astype(BF16)
                d = jnp.dot(pb[h * _ROWS_H:(h + 1) * _ROWS_H], vh, preferred_element_type=F32)
                pv[h] = d if pv[h] is None else pv[h] + d
        l_scr[...] = l_new
        acc_scr[...] = alpha * acc_scr[...] + jnp.concatenate(pv, axis=0)
        m_scr[...] = m_new

    update(k_refs, v_refs, False)

    @pl.when(pc == n_steps - 1)
    def _():
        update([kn_ref], [vn_ref], True)
        on = acc_scr[...] / l_scr[...]
        lam = _lambda_full(lq_ref[...], lam_init)
        for h in range(N_KV_HEADS):
            blk = on[h * _ROWS_H:(h + 1) * _ROWS_H]
            od = blk[0:_ROWS_H // 2] - lam * blk[_ROWS_H // 2:_ROWS_H]
            o_ref[h * (_ROWS_H // 2):(h + 1) * (_ROWS_H // 2), :] = _subln(od, sg_ref[...], lam_init)


def _dattn(page_table, qbd, cache_kt, cache_v2, knew_t, vnew2, lq, sg, layer, lam_init):
    db, n_pages = page_table.shape
    npg = PAGES_PER_STEP
    n_steps = n_pages // npg
    rows, page = cache_kt.shape[2], cache_kt.shape[3]

    def page_spec(i):
        return pl.BlockSpec((None, None, rows, page),
                            lambda b, pc, pt: (layer, pt[b, pc * npg + i], 0, 0))

    per_b = lambda r, w: pl.BlockSpec((None, r, w), lambda b, pc, pt: (b, 0, 0))
    const = lambda shape: pl.BlockSpec(shape, lambda b, pc, pt: (0,) * len(shape))
    grid_spec = pltpu.PrefetchScalarGridSpec(
        num_scalar_prefetch=1,
        grid=(db, n_steps),
        in_specs=[per_b(_DEC_ROWS, rows)] + [page_spec(i) for i in range(npg)]
        + [page_spec(i) for i in range(npg)]
        + [per_b(rows, page), per_b(rows, page), const((4, HEAD_DIM)), const((1, LANES))],
        out_specs=per_b(_DEC_ROWS // 2, LANES),
        scratch_shapes=[pltpu.VMEM((_DEC_ROWS, LANES), F32), pltpu.VMEM((_DEC_ROWS, LANES), F32),
                        pltpu.VMEM((_DEC_ROWS, LANES), F32)],
    )
    return pl.pallas_call(
        functools.partial(_dattn_kernel, lam_init=lam_init, n_steps=n_steps),
        grid_spec=grid_spec,
        out_shape=jax.ShapeDtypeStruct((db, _DEC_ROWS // 2, LANES), F32),
        compiler_params=_cparams(("parallel", "arbitrary")),
        name="dattn",
    )(page_table, qbd, *([cache_kt] * npg), *([cache_v2] * npg), knew_t, vnew2, lq, sg)


def _gelu_tanh(x):
    return 0.5 * x * (1.0 + jnp.tanh(math.sqrt(2.0 / math.pi) * (x + 0.044715 * (x * x * x))))


def _softplus(x):
    return jnp.maximum(x, 0.0) + jnp.log(1.0 + jnp.exp(-jnp.abs(x)))


def _lru_block(x, shifted, gr, h_in, seg_row, seg_len, cw, cb, wax, ba, bx, lam):
    xc = cb + (((shifted[2] * cw[0:1] + shifted[1] * cw[1:2]) + shifted[0] * cw[2:3]) + x * cw[3:4])
    gates = jnp.dot(xc.astype(BF16), wax, preferred_element_type=F32)
    r = _sigmoid(gates[:, :LRU_BLOCK] + ba)
    i = _sigmoid(gates[:, LRU_BLOCK:] + bx)
    log_a = (-LRU_C * _softplus(-lam)) * r
    a = jnp.exp(log_a)
    b = jnp.sqrt(1.0 - jnp.exp(2.0 * log_a)) * (i * xc)
    d = 1
    while d < seg_len:
        if d < SUBLANES:
            keep = seg_row >= d
            a_sh = pltpu.roll(a, d, 0)
            b_sh = pltpu.roll(b, d, 0)
            b = jnp.where(keep, a * b_sh + b, b)
            a = jnp.where(keep, a * a_sh, a)
        else:
            a_sh = jnp.concatenate([jnp.ones((d, LANES), F32), a[:-d]], axis=0)
            b_sh = jnp.concatenate([jnp.zeros((d, LANES), F32), b[:-d]], axis=0)
            b = a * b_sh + b
            a = a * a_sh
        d *= 2
    h = a * h_in + b
    return h, h * _gelu_tanh(gr)


def _lru_sample_kernel(xr_ref, gr_ref, p1_ref, p2_ref, p3_ref, h0_ref, cw_ref, cb_ref, wax_ref,
                       ba_ref, bx_ref, lam_ref, y_ref, h_ref, *, seg_len):
    tm = xr_ref.shape[0]
    seg_row = lax.broadcasted_iota(jnp.int32, (tm, LANES), 0) & (seg_len - 1)
    prevs = (p1_ref, p2_ref, p3_ref)
    for n in range(N_LRU_BLOCKS):
        cs = slice(n * LANES, (n + 1) * LANES)
        x = xr_ref[:, cs]
        shifted = [jnp.where(seg_row < j, prevs[j - 1][:, cs], pltpu.roll(x, j, 0))
                   for j in range(1, CONV_W)]
        h, y = _lru_block(x, shifted, gr_ref[:, cs], h0_ref[:, cs], seg_row, seg_len,
                          cw_ref[:, cs], cb_ref[:, cs], wax_ref[n], ba_ref[:, cs], bx_ref[:, cs],
                          lam_ref[:, cs])
        y_ref[:, cs] = y.astype(BF16)
        h_ref[:, cs] = h


def _lru_sample(xr, gr, prevs, h0, cw, cb, wax, ba, bx, lam, seg_len):
    n = xr.shape[0]
    full = lambda shape: pl.BlockSpec(shape, lambda i: (0,) * len(shape))
    return pl.pallas_call(
        functools.partial(_lru_sample_kernel, seg_len=seg_len),
        grid=(1,),
        in_specs=[full((n, D_RNN))] * 6 + [full((CONV_W, D_RNN)), full((1, D_RNN)),
                                            full((N_LRU_BLOCKS, LRU_BLOCK, 2 * LRU_BLOCK)),
                                            full((1, D_RNN)), full((1, D_RNN)), full((1, D_RNN))],
        out_specs=[full((n, D_RNN)), full((n, D_RNN))],
        out_shape=[jax.ShapeDtypeStruct((n, D_RNN), BF16), jax.ShapeDtypeStruct((n, D_RNN), F32)],
        compiler_params=_cparams(("arbitrary",)),
        name="lru_sample",
    )(xr, gr, *prevs, h0, cw, cb, wax, ba, bx, lam)


def _merge_kernel(o_ref, *refs, tiles_per_seq):
    if tiles_per_seq is None:
        y_ref, gt_ref, x_ref, woa_ref, wol_ref, wout_ref, out_ref = refs
        o = o_ref[...]
    else:
        ol_ref, y_ref, gt_ref, x_ref, woa_ref, wol_ref, wout_ref, out_ref = refs
        is_last = pl.program_id(0) % tiles_per_seq == tiles_per_seq - 1
        o = jnp.where(is_last, ol_ref[...], o_ref[...])
    ya = jnp.dot(o, woa_ref[...], preferred_element_type=F32)
    yl = jnp.dot(y_ref[...], wol_ref[...], preferred_element_type=F32)
    merged = _sigmoid(gt_ref[:, :D_MODEL]) * ya + _sigmoid(gt_ref[:, D_MODEL:]) * yl
    out_ref[...] = x_ref[...] + jnp.dot(merged.astype(BF16), wout_ref[...],
                                        preferred_element_type=F32)


def _merge(o, y, gt, x, woa, wol, wout, tiles_per_seq=None):
    n = x.shape[0]
    tm = TOK_TILE
    row = lambda width: pl.BlockSpec((tm, width), lambda i: (i, 0))
    if tiles_per_seq is None:
        o_specs, o_args = [row(Q_W)], [o]
    else:
        nt = tiles_per_seq
        o_specs = [pl.BlockSpec((tm, Q_W),
                                lambda i: ((i // nt) * (nt - 1) + jnp.minimum(i % nt, nt - 2), 0)),
                   pl.BlockSpec((tm, Q_W), lambda i: (i // nt, 0))]
        o_args = list(o)
    return pl.pallas_call(
        functools.partial(_merge_kernel, tiles_per_seq=tiles_per_seq),
        grid=(n // tm,),
        in_specs=o_specs + [row(D_RNN), row(2 * D_MODEL), row(D_MODEL),
                            _resident((Q_W, D_MODEL)), _resident((D_RNN, D_MODEL)),
                            _resident((D_MODEL, D_MODEL))],
        out_specs=row(D_MODEL),
        out_shape=jax.ShapeDtypeStruct((n, D_MODEL), F32),
        compiler_params=_cparams(("parallel",)),
        name="merge",
    )(*o_args, y, gt, x, woa, wol, wout)


_FF_CHUNK = 256


def _ffn_kernel(x_ref, g_ref, wgu_ref, wd_ref, out_ref, act_scr):
    x = x_ref[...]
    h = _rmsnorm_rows(x, g_ref[...]).astype(BF16)
    for c0 in range(0, D_FF, _FF_CHUNK):
        gate = jnp.dot(h, wgu_ref[:, c0:c0 + _FF_CHUNK], preferred_element_type=F32)
        up = jnp.dot(h, wgu_ref[:, D_FF + c0:D_FF + c0 + _FF_CHUNK], preferred_element_type=F32)
        act_scr[:, c0:c0 + _FF_CHUNK] = (gate * _sigmoid(gate) * up).astype(BF16)
    out_ref[...] = x + jnp.dot(act_scr[...], wd_ref[...], preferred_element_type=F32)


def _ffn(x, g, wgu, wd):
    n = x.shape[0]
    tm = TOK_TILE
    row = pl.BlockSpec((tm, D_MODEL), lambda i: (i, 0))
    return pl.pallas_call(
        _ffn_kernel,
        grid=(n // tm,),
        in_specs=[row, _resident((1, D_MODEL)), _resident((D_MODEL, 2 * D_FF)),
                  _resident((D_FF, D_MODEL))],
        out_specs=row,
        out_shape=jax.ShapeDtypeStruct((n, D_MODEL), F32),
        scratch_shapes=[pltpu.VMEM((tm, D_FF), BF16)],
        compiler_params=_cparams(("parallel",)),
        name="ffn",
    )(x, g, wgu, wd)


def _rope_tables(pos):
    inv_freq = 1.0 / (ROPE_THETA ** (jnp.arange(0, HEAD_DIM, 2, dtype=F32) / HEAD_DIM))
    ang = pos.astype(F32)[:, None] * inv_freq[None, :]
    cos = jnp.cos(ang)
    sin = jnp.sin(ang)
    return jnp.tile(cos, (1, 4)), jnp.tile(jnp.concatenate([-sin, sin], axis=-1), (1, 2))


def kernel(x_prompt, x_sample, cache_k, cache_v, state_h, state_conv, page_table, meta_tokens,
           norm1_g, w_in, q_norm_g, k_norm_g, lambda_qk, subln_g, w_o_attn,
           conv_w, conv_b, lru_w_a, lru_b_a, lru_w_x, lru_b_x, lru_lambda, w_o_lru,
           w_out, norm2_g, w_gu, w_down):
    n_batch, seq, _ = x_prompt.shape
    db, dec_seq, _ = x_sample.shape
    depth = w_in.shape[0]
    n_pages = page_table.shape[1]
    page = cache_k.shape[2]
    past_len = n_pages * page
    t_real = seq + N_META
    t_pad = -(-t_real // ATT_BLOCK) * ATT_BLOCK
    assert ATT_BLOCK == TOK_TILE and dec_seq == SUBLANES and (db * dec_seq) % TOK_TILE == 0
    assert page == LANES and n_pages % PAGES_PER_STEP == 0

    xp = jnp.pad(x_prompt, ((0, 0), (N_META, t_pad - t_real), (0, 0)))
    xp = lax.dynamic_update_slice(
        xp, jnp.broadcast_to(meta_tokens.astype(F32)[None], (n_batch, N_META, D_MODEL)), (0, 0, 0))
    xp = xp.reshape(n_batch * t_pad, D_MODEL)
    xs = x_sample.reshape(db * dec_seq, D_MODEL)

    cos_p, sin_p = _rope_tables(jnp.arange(t_pad, dtype=jnp.int32))
    pos_s = past_len + (jnp.arange(db * dec_seq, dtype=jnp.int32) % dec_seq)
    cos_s, sin_s = _rope_tables(pos_s)

    ckt = jnp.transpose(cache_k, (0, 1, 3, 4, 5, 2)).reshape(cache_k.shape[0], cache_k.shape[1], KV_W, page)
    cv2 = cache_v.reshape(cache_v.shape[0], cache_v.shape[1], page * N_KV_HEADS, 2 * HEAD_DIM)
    eye_h = jnp.eye(N_KV_HEADS, dtype=BF16)
    eye_c = jnp.eye(2, dtype=BF16)

    outs = {k: [] for k in ("kp", "vp", "hp", "cp", "ks", "vs", "hs", "cs")}
    for l in range(depth):
        lam_init = 0.8 - 0.6 * math.exp(-0.3 * l)
        g1 = norm1_g[l][None]
        win = w_in[l].astype(BF16)
        gq = jnp.tile(q_norm_g[l], 2)[None]
        gk = jnp.tile(k_norm_g[l], 2)[None]
        lq = lambda_qk[l]
        sg = subln_g[l][None]
        woa = w_o_attn[l].astype(BF16)
        cw = conv_w[l]
        cb = conv_b[l][None]
        wax = jnp.concatenate([lru_w_a[l], lru_w_x[l]], axis=-1).astype(BF16)
        ba = lru_b_a[l][None]
        bx = lru_b_x[l][None]
        lam_l = lru_lambda[l][None]
        wol = w_o_lru[l].astype(BF16)
        wout = w_out[l].astype(BF16)
        g2 = norm2_g[l][None]
        wgu = w_gu[l].astype(BF16)
        wd = w_down[l].astype(BF16)

        q, k, kb, v, vb, gt, y, hl, cvn = _inproj_prompt(xp, g1, win, cos_p, sin_p, gq, gk, cw, cb, wax, ba, bx,
                                                         lam_l, n_batch, t_pad, t_real)
        o = _pattn(q, kb, vb, lq, sg, n_batch, t_pad, t_real, lam_init)
        xp = _ffn(_merge(o, y, gt, xp, woa, wol, wout, t_pad // TOK_TILE), g2, wgu, wd)
        outs["kp"].append(jnp.transpose(k.reshape(n_batch, N_KV_HEADS, 2, HEAD_DIM, t_real), (0, 4, 1, 2, 3)))
        outs["vp"].append(v.reshape(n_batch, t_real, N_KV_HEADS, 2 * HEAD_DIM))
        outs["hp"].append(hl[:, SUBLANES - 1])
        outs["cp"].append(cvn[:, SUBLANES - (CONV_W - 1):])

        q, k, v, xr, gr, gt = _inproj_sample(xs, g1, win, cos_s, sin_s, gq, gk)
        q6 = q.reshape(db, dec_seq, N_KV_HEADS, GROUP, 2, HEAD_DIM)
        qr = jnp.transpose(q6, (0, 2, 4, 3, 1, 5)).reshape(db, N_KV_HEADS, 2, GROUP * dec_seq, HEAD_DIM)
        qbd = jnp.einsum("bhcrd,hx,cy->bhcrxyd", qr, eye_h, eye_c).reshape(db, _DEC_ROWS, KV_W)
        knew_t = jnp.pad(jnp.transpose(k.reshape(db, dec_seq, KV_W), (0, 2, 1)),
                         ((0, 0), (0, 0), (0, page - dec_seq)))
        vnew2 = jnp.pad(v.reshape(db, dec_seq * N_KV_HEADS, 2 * HEAD_DIM),
                        ((0, 0), (0, (page - dec_seq) * N_KV_HEADS), (0, 0)))
        od = _dattn(page_table, qbd, ckt, cv2, knew_t, vnew2, lq, sg, l, lam_init)
        o = jnp.transpose(od.reshape(db, N_KV_HEADS, GROUP, dec_seq, 2 * HEAD_DIM),
                          (0, 3, 1, 2, 4)).reshape(db * dec_seq, Q_W).astype(BF16)
        conv_ext = jnp.concatenate([state_conv[l], jnp.zeros((db, dec_seq, D_RNN), F32)], axis=1)
        prevs = [conv_ext[:, CONV_W - 1 - j:CONV_W - 1 - j + dec_seq].reshape(db * dec_seq, D_RNN)
                 for j in range(1, CONV_W)]
        h0 = jnp.repeat(state_h[l], dec_seq, axis=0)
        y, hs = _lru_sample(xr, gr, prevs, h0, cw, cb, wax, ba, bx, lam_l, dec_seq)
        xs = _ffn(_merge(o, y, gt, xs, woa, wol, wout), g2, wgu, wd)
        outs["ks"].append(k.reshape(db, dec_seq, N_KV_HEADS, 2, HEAD_DIM))
        outs["vs"].append(v.reshape(db, dec_seq, N_KV_HEADS, 2 * HEAD_DIM))
        outs["hs"].append(hs.reshape(db, dec_seq, D_RNN)[:, -1])
        outs["cs"].append(xr.reshape(db, dec_seq, D_RNN)[:, dec_seq - (CONV_W - 1):])

    y_prompt = xp.reshape(n_batch, t_pad, D_MODEL)[:, N_META:t_real]
    y_sample = xs.reshape(db, dec_seq, D_MODEL)
    return (y_prompt, y_sample,
            jnp.stack(outs["kp"]), jnp.stack(outs["vp"]), jnp.stack(outs["hp"]), jnp.stack(outs["cp"]),
            jnp.stack(outs["ks"]), jnp.stack(outs["vs"]), jnp.stack(outs["hs"]), jnp.stack(outs["cs"]))
```

```python
import functools
import math

import jax
import jax.numpy as jnp
from jax import lax
from jax.experimental import pallas as pl
from jax.experimental.pallas import tpu as pltpu

F32 = jnp.float32
BF16 = jnp.bfloat16

D_MODEL = 1024
N_META = 16
N_KV_HEADS = 4
GROUP = 2
HEAD_DIM = 64
Q_W = 1024
KV_W = 512
D_RNN = 1280
N_LRU_BLOCKS = 10
LRU_BLOCK = 128
CONV_W = 4
LRU_C = 8.0
D_FF = 2816
N_IN = Q_W + 2 * KV_W + 2 * D_RNN + 2 * D_MODEL
ROPE_THETA = 10000.0
NORM_EPS = 1e-6

LANES = 128
SUBLANES = 8
TOK_TILE = 256
ATT_BLOCK = 256
PAGES_PER_STEP = 16
LAST_ROW_IN_TILE = 15
Q_TILE_BLOCKS = 2
KEY_BLOCKS_PER_CHUNK = 4
VMEM_LIMIT = 48 * 1024 * 1024


def _cparams(sem):
    return pltpu.CompilerParams(dimension_semantics=sem, vmem_limit_bytes=VMEM_LIMIT)


def _resident(shape):
    return pl.BlockSpec(shape, lambda *_: (0,) * len(shape), pipeline_mode=pl.Buffered(1))


def _rmsnorm_rows(x, g):
    ms = jnp.mean(x * x, axis=-1, keepdims=True)
    return x * lax.rsqrt(ms + NORM_EPS) * g


def _sigmoid(x):
    return 1.0 / (1.0 + jnp.exp(-x))


_V_START = Q_W + KV_W
_REST_SPLITS = ((Q_W + 2 * KV_W, D_RNN), (Q_W + 2 * KV_W + D_RNN, D_RNN),
                (Q_W + 2 * KV_W + 2 * D_RNN, 2 * D_MODEL))
_N_CHUNK = 512


def _norm_rope(x, cos, sin, g, low, first_half):
    ss = x * x
    s_low = jnp.sum(jnp.where(low, ss, 0.0), axis=-1, keepdims=True)
    s_high = jnp.sum(jnp.where(low, 0.0, ss), axis=-1, keepdims=True)
    ms = jnp.where(low, s_low, s_high) * (1.0 / HEAD_DIM)
    y = x * lax.rsqrt(ms + NORM_EPS) * g
    partner = jnp.where(first_half, pltpu.roll(y, LANES - HEAD_DIM // 2, 1),
                        pltpu.roll(y, HEAD_DIM // 2, 1))
    return y * cos + partner * sin


def _inproj_kernel(x_ref, g_ref, w_ref, cos_ref, sin_ref, gq_ref, gk_ref, *refs, prompt, tiles_per_seq,
                   n_alias=0):
    if prompt:
        refs = refs[:6] + refs[6 + n_alias:]
        (cw_ref, cb_ref, wax_ref, ba_ref, bx_ref, lam_ref,
         q_ref, k_ref, kb_ref, v_ref, vb_ref, gt_ref, y_ref, hl_ref, cv_ref,
         xr_ref, gr_ref, xcar, hcar) = refs
    else:
        q_ref, k_ref, v_ref, xr_ref, gr_ref, gt_ref = refs
        kb_ref = vb_ref = None
    tm = x_ref.shape[0]
    h = _rmsnorm_rows(x_ref[...], g_ref[...]).astype(BF16)

    def project(ref, start, width, c0):
        cw = min(_N_CHUNK, width - c0)
        ref[:, c0:c0 + cw] = jnp.dot(h, w_ref[:, start + c0:start + c0 + cw],
                                     preferred_element_type=F32)

    for ref, (start, width) in zip((xr_ref, gr_ref), _REST_SPLITS[:2]):
        for c0 in range(0, width, _N_CHUNK):
            project(ref, start, width, c0)

    lane = lax.broadcasted_iota(jnp.int32, (tm, LANES), 1)
    low = lane < HEAD_DIM
    first_half = (lane & (HEAD_DIM - 1)) < HEAD_DIM // 2
    n_q = Q_W // LANES

    def qk_chunk(c0):
        z = jnp.dot(h, w_ref[:, c0:c0 + _N_CHUNK], preferred_element_type=F32)
        for jb in range(_N_CHUNK // LANES):
            j = c0 // LANES + jb
            out = _norm_rope(z[:, jb * LANES:(jb + 1) * LANES], cos_ref[...], sin_ref[...],
                             gq_ref[...] if j < n_q else gk_ref[...], low, first_half)
            if j < n_q:
                q_ref[:, j * LANES:(j + 1) * LANES] = (out * (HEAD_DIM ** -0.5)).astype(BF16)
            else:
                cs = slice((j - n_q) * LANES, (j - n_q + 1) * LANES)
                if prompt:
                    k_ref[cs, :] = out.T
                    kb_ref[:, cs] = out.astype(BF16)
                else:
                    k_ref[:, cs] = out

    def v_chunk():
        z = jnp.dot(h, w_ref[:, _V_START:_V_START + KV_W], preferred_element_type=F32)
        if prompt:
            for hh in range(N_KV_HEADS):
                v_ref[pl.ds(hh, tm, stride=N_KV_HEADS), :] = z[:, hh * LANES:(hh + 1) * LANES]
            vb_ref[...] = z.astype(BF16)
        else:
            v_ref[...] = z

    gt_start, gt_width = _REST_SPLITS[2]
    work = [functools.partial(qk_chunk, c0) for c0 in range(0, Q_W + KV_W, _N_CHUNK)] + [v_chunk]
    work += [functools.partial(project, gt_ref, gt_start, gt_width, c0)
             for c0 in range(0, gt_width, _N_CHUNK)]
    if not prompt:
        for item in work:
            item()
        return

    first_tile = pl.program_id(0) % tiles_per_seq == 0
    row = lax.broadcasted_iota(jnp.int32, (tm, LANES), 0)
    row8 = lax.broadcasted_iota(jnp.int32, (SUBLANES, LANES), 0)
    for n in range(N_LRU_BLOCKS):
        cs = slice(n * LANES, (n + 1) * LANES)
        x = xr_ref[:, cs]
        car = jnp.where(first_tile, 0.0, xcar[:, cs])
        h_in = jnp.where(first_tile, 0.0, hcar[0:1, cs])
        shifted = []
        for j in range(1, CONV_W):
            main = pltpu.roll(x, j, 0)
            head = jnp.where(row8 < j, pltpu.roll(car, j, 0), main[0:SUBLANES])
            shifted.append(jnp.concatenate([head, main[SUBLANES:]], axis=0))
        hs, y = _lru_block(x, shifted, gr_ref[:, cs], h_in, row, tm,
                           cw_ref[:, cs], cb_ref[:, cs], wax_ref[n], ba_ref[:, cs], bx_ref[:, cs],
                           lam_ref[:, cs])
        y_ref[:, cs] = y.astype(BF16)
        xcar[:, cs] = x[tm - SUBLANES:tm]
        hcar[0:1, cs] = hs[tm - 1:tm]
        r0 = LAST_ROW_IN_TILE - (SUBLANES - 1)
        hl_ref[:, cs] = hs[r0:r0 + SUBLANES]
        cv_ref[:, cs] = x[r0:r0 + SUBLANES]
        if n < len(work):
            work[n]()
    for item in work[N_LRU_BLOCKS:]:
        item()


def _inproj_prompt(x, g, w, cos, sin, gq, gk, cw, cb, wax, ba, bx, lam, n_batch, t_pad, t_real,
                   layer, depth, kv_all):
    n = x.shape[0]
    tm = TOK_TILE
    nt = t_pad // tm
    last_tile, last_row = divmod(t_real - 1, tm)
    assert last_tile == nt - 1 and last_row == LAST_ROW_IN_TILE
    row = lambda width: pl.BlockSpec((tm, width), lambda i: (i, 0))
    tab = pl.BlockSpec((tm, LANES), lambda i: (i % nt, 0))
    tail = pl.BlockSpec((None, SUBLANES, D_RNN), lambda i: (i // nt, 0, 0))
    outs = [(Q_W, BF16), None, (KV_W, BF16), None, (KV_W, BF16), (2 * D_MODEL, F32), (D_RNN, BF16)]
    specs = [row(w_[0]) if w_ else None for w_ in outs]
    shapes = [jax.ShapeDtypeStruct((n, w_[0]), w_[1]) if w_ else None for w_ in outs]
    specs[1] = pl.BlockSpec((None, None, KV_W, tm), lambda i: (layer, i // nt, 0, i % nt))
    shapes[1] = jax.ShapeDtypeStruct((depth, n_batch, KV_W, t_real), F32)
    specs[3] = pl.BlockSpec((None, None, tm * N_KV_HEADS, LANES), lambda i: (layer, i // nt, i % nt, 0))
    shapes[3] = jax.ShapeDtypeStruct((depth, n_batch, t_real * N_KV_HEADS, LANES), F32)
    n_fixed = 13
    alias_in = [] if kv_all is None else list(kv_all)
    aliases = {n_fixed + j: o for j, o in enumerate((1, 3)[:len(alias_in)])}
    return pl.pallas_call(
        functools.partial(_inproj_kernel, prompt=True, tiles_per_seq=nt, n_alias=len(alias_in)),
        input_output_aliases=aliases,
        grid=(n // tm,),
        in_specs=[row(D_MODEL), _resident((1, D_MODEL)), _resident((D_MODEL, N_IN)), tab, tab,
                  _resident((1, LANES)), _resident((1, LANES)),
                  _resident((CONV_W, D_RNN)), _resident((1, D_RNN)),
                  _resident((N_LRU_BLOCKS, LRU_BLOCK, 2 * LRU_BLOCK)),
                  _resident((1, D_RNN)), _resident((1, D_RNN)), _resident((1, D_RNN))]
        + [pl.BlockSpec(memory_space=pl.ANY)] * len(alias_in),
        out_specs=specs + [tail, tail],
        out_shape=shapes + [jax.ShapeDtypeStruct((n_batch, SUBLANES, D_RNN), F32)] * 2,
        scratch_shapes=[pltpu.VMEM((tm, D_RNN), F32), pltpu.VMEM((tm, D_RNN), F32),
                        pltpu.VMEM((SUBLANES, D_RNN), F32), pltpu.VMEM((SUBLANES, D_RNN), F32)],
        compiler_params=_cparams(("arbitrary",)),
        name="inproj_prompt",
    )(x, g, w, cos, sin, gq, gk, cw, cb, wax, ba, bx, lam, *alias_in)


def _inproj_sample(x, g, w, cos, sin, gq, gk):
    n = x.shape[0]
    tm = TOK_TILE
    row = lambda width: pl.BlockSpec((tm, width), lambda i: (i, 0))
    tab = row(LANES)
    outs = [(Q_W, BF16), (KV_W, F32), (KV_W, F32), (D_RNN, F32), (D_RNN, F32), (2 * D_MODEL, F32)]
    return pl.pallas_call(
        functools.partial(_inproj_kernel, prompt=False, tiles_per_seq=None),
        grid=(n // tm,),
        in_specs=[row(D_MODEL), _resident((1, D_MODEL)), _resident((D_MODEL, N_IN)), tab, tab,
                  _resident((1, LANES)), _resident((1, LANES))],
        out_specs=[row(w_) for w_, _ in outs],
        out_shape=[jax.ShapeDtypeStruct((n, w_), dt) for w_, dt in outs],
        compiler_params=_cparams(("parallel",)),
        name="inproj_sample",
    )(x, g, w, cos, sin, gq, gk)


def _lambda_full(lq, lam_init):
    a = jnp.sum(lq[0:1, :] * lq[1:2, :], axis=-1, keepdims=True)
    b = jnp.sum(lq[2:3, :] * lq[3:4, :], axis=-1, keepdims=True)
    return jnp.exp(a) - jnp.exp(b) + lam_init


def _subln(o, g, lam_init):
    return _rmsnorm_rows(o, g) * (1.0 - lam_init)


def _pattn_kernel(q_ref, kb_ref, vb_ref, lq_ref, sg_ref, o_ref, qs_scr, m_scr, l_scr, acc_scr,
                  *, lam_init, tq, q_block):
    tb = ATT_BLOCK
    rows = 2 * GROUP * tq
    q_blocks = max(tq // tb, 1)
    qi = pl.program_id(2) * q_blocks if q_block is None else q_block
    lane = lax.broadcasted_iota(jnp.int32, (tq, LANES), 1)
    low = lane < HEAD_DIM
    for c in range(2):
        for g in range(GROUP):
            qg = q_ref[:, g * LANES:(g + 1) * LANES].astype(F32)
            r0 = (c * GROUP + g) * tq
            sel = jnp.where(low, qg, 0.0) if c == 0 else jnp.where(low, 0.0, qg)
            qs_scr[r0:r0 + tq, :] = sel.astype(BF16)
    m_scr[...] = jnp.full(m_scr.shape, -jnp.inf, F32)
    l_scr[...] = jnp.zeros(l_scr.shape, F32)
    acc_scr[...] = jnp.zeros(acc_scr.shape, F32)

    def update(start, n_keys, first_visible):
        kc = kb_ref[pl.ds(start, n_keys), :]
        vc = vb_ref[pl.ds(start, n_keys), :]
        s = lax.dot_general(qs_scr[...], kc, (((1,), (1,)), ((), ())),
                            preferred_element_type=F32)
        if first_visible is not None:
            t = lax.broadcasted_iota(jnp.int32, s.shape, 0) & (tq - 1)
            col = lax.broadcasted_iota(jnp.int32, s.shape, 1)
            s = jnp.where(col <= t + first_visible, s, -jnp.inf)
        m_prev = m_scr[...]
        m_new = jnp.maximum(m_prev, jnp.max(s, axis=1, keepdims=True))
        alpha = jnp.exp(m_prev - m_new)
        p = jnp.exp(s - jnp.concatenate([m_new] * (n_keys // LANES), axis=1))
        l_scr[...] = alpha * l_scr[...] + jnp.sum(p, axis=1, keepdims=True)
        acc_scr[...] = alpha * acc_scr[...] + jnp.dot(p.astype(BF16), vc,
                                                      preferred_element_type=F32)
        m_scr[...] = m_new

    nb = KEY_BLOCKS_PER_CHUNK

    def body(j, carry):
        update(pl.multiple_of(j * nb * tb, nb * tb), nb * tb, None)
        return carry

    lax.fori_loop(0, qi // nb, body, 0)

    if q_block is None:
        for r in range(0, nb, q_blocks):
            @pl.when(qi % nb == r)
            def _(r=r):
                update(pl.multiple_of((qi - r) * tb, tb), (r + q_blocks) * tb, r * tb)
    else:
        r = q_block % nb
        update((q_block - r) * tb, (r + q_blocks) * tb, r * tb)

    on = acc_scr[...] / l_scr[...]
    lam = _lambda_full(lq_ref[...], lam_init)
    od = on[0:rows // 2] - lam * on[rows // 2:rows]
    res = _subln(od, sg_ref[...], lam_init).astype(BF16)
    if tq < o_ref.shape[0]:
        o_ref[...] = jnp.zeros(o_ref.shape, BF16)
    for g in range(GROUP):
        o_ref[0:tq, g * LANES:(g + 1) * LANES] = res[g * tq:(g + 1) * tq]


def _pattn(q, kb, vb, lq, sg, n_batch, t_pad, t_real, lam_init):
    tb = ATT_BLOCK
    nq = t_pad // tb
    n_last = t_real - (nq - 1) * tb
    assert 0 < n_last <= tb and n_last % (2 * SUBLANES) == 0
    kv = pl.BlockSpec((t_pad, LANES), lambda b, h, *i: (b, h))

    q3 = q.reshape(n_batch, t_pad, Q_W)

    def call(tq, q_block, grid, q_map, o_map, o_rows, name):
        rows = 2 * GROUP * tq
        return pl.pallas_call(
            functools.partial(_pattn_kernel, lam_init=lam_init, tq=tq, q_block=q_block),
            grid=grid,
            in_specs=[pl.BlockSpec((None, tq, GROUP * LANES), q_map), kv, kv,
                      _resident((4, HEAD_DIM)), _resident((1, LANES))],
            out_specs=pl.BlockSpec((max(tq, tb), GROUP * LANES), o_map),
            out_shape=jax.ShapeDtypeStruct((o_rows, Q_W), BF16),
            scratch_shapes=[pltpu.VMEM((rows, LANES), BF16),
                            pltpu.VMEM((rows, LANES), F32),
                            pltpu.VMEM((rows, LANES), F32),
                            pltpu.VMEM((rows, LANES), F32)],
            compiler_params=_cparams(("parallel",) * (len(grid) - 1) + ("arbitrary",)),
            name=name,
        )(q3, kb, vb, lq, sg)

    tq = Q_TILE_BLOCKS * tb
    n_main = (nq - 1) * tb // tq
    assert n_main * tq == (nq - 1) * tb and KEY_BLOCKS_PER_CHUNK % Q_TILE_BLOCKS == 0
    o_main = call(tq, None, (n_batch, N_KV_HEADS, n_main),
                  lambda b, h, i: (b, i, h), lambda b, h, i: (b * n_main + i, h),
                  n_batch * (nq - 1) * tb, "pattn")
    o_last = call(n_last, nq - 1, (n_batch, N_KV_HEADS),
                  lambda b, h: (b, (nq - 1) * tb // n_last, h), lambda b, h: (b, h),
                  n_batch * tb, "pattn_last")
    return o_main, o_last


_DEC_ROWS = N_KV_HEADS * 2 * GROUP * 8
_ROWS_H = _DEC_ROWS // N_KV_HEADS


def _dattn_kernel(pt_ref, qb_ref, *refs, lam_init, n_steps):
    npg = PAGES_PER_STEP
    k_refs = refs[:npg]
    v_refs = refs[npg:2 * npg]
    kn_ref, vn_ref, lq_ref, sg_ref, o_ref, m_scr, l_scr, acc_scr = refs[2 * npg:]
    pc = pl.program_id(1)

    @pl.when(pc == 0)
    def _():
        m_scr[...] = jnp.full(m_scr.shape, -jnp.inf, F32)
        l_scr[...] = jnp.zeros(l_scr.shape, F32)
        acc_scr[...] = jnp.zeros(acc_scr.shape, F32)

    qb = qb_ref[...]

    def update(pages_k, pages_v, masked):
        ss = []
        for kp in pages_k:
            s = jnp.dot(qb, kp[...].astype(BF16), preferred_element_type=F32)
            if masked:
                row = lax.broadcasted_iota(jnp.int32, s.shape, 0)
                col = lax.broadcasted_iota(jnp.int32, s.shape, 1)
                s = jnp.where(col <= (row & 7), s, -jnp.inf)
            ss.append(s)
        m_prev = m_scr[...]
        m_cur = ss[0]
        for s in ss[1:]:
            m_cur = jnp.maximum(m_cur, s)
        m_new = jnp.maximum(m_prev, jnp.max(m_cur, axis=1, keepdims=True))
        alpha = jnp.exp(m_prev - m_new)
        l_new = alpha * l_scr[...]
        pv = [None] * N_KV_HEADS
        n_keys = pages_v[0].shape[0] // N_KV_HEADS
        for s, vp in zip(ss, pages_v):
            p = jnp.exp(s - m_new)
            l_new = l_new + jnp.sum(p, axis=1, keepdims=True)
            pb = p.astype(BF16)
            for h in range(N_KV_HEADS):
                vh = vp[pl.ds(h, n_keys, stride=N_KV_HEADS), :].astype(BF16)
                d = jnp.dot(pb[h * _ROWS_H:(h + 1) * _ROWS_H], vh, preferred_element_type=F32)
                pv[h] = d if pv[h] is None else pv[h] + d
        l_scr[...] = l_new
        acc_scr[...] = alpha * acc_scr[...] + jnp.concatenate(pv, axis=0)
        m_scr[...] = m_new

    update(k_refs, v_refs, False)

    @pl.when(pc == n_steps - 1)
    def _():
        update([kn_ref], [vn_ref], True)
        on = acc_scr[...] / l_scr[...]
        lam = _lambda_full(lq_ref[...], lam_init)
        for h in range(N_KV_HEADS):
            blk = on[h * _ROWS_H:(h + 1) * _ROWS_H]
            od = blk[0:_ROWS_H // 2] - lam * blk[_ROWS_H // 2:_ROWS_H]
            o_ref[h * (_ROWS_H // 2):(h + 1) * (_ROWS_H // 2), :] = _subln(od, sg_ref[...], lam_init)


def _dattn(page_table, qbd, cache_kt, cache_v2, knew_t, vnew2, lq, sg, layer, lam_init):
    db, n_pages = page_table.shape
    npg = PAGES_PER_STEP
    n_steps = n_pages // npg
    rows, page = cache_kt.shape[2], cache_kt.shape[3]

    def page_spec(i):
        return pl.BlockSpec((None, None, rows, page),
                            lambda b, pc, pt: (layer, pt[b, pc * npg + i], 0, 0))

    per_b = lambda r, w: pl.BlockSpec((None, r, w), lambda b, pc, pt: (b, 0, 0))
    const = lambda shape: pl.BlockSpec(shape, lambda b, pc, pt: (0,) * len(shape))
    grid_spec = pltpu.PrefetchScalarGridSpec(
        num_scalar_prefetch=1,
        grid=(db, n_steps),
        in_specs=[per_b(_DEC_ROWS, rows)] + [page_spec(i) for i in range(npg)]
        + [page_spec(i) for i in range(npg)]
        + [per_b(rows, page), per_b(rows, page), const((4, HEAD_DIM)), const((1, LANES))],
        out_specs=per_b(_DEC_ROWS // 2, LANES),
        scratch_shapes=[pltpu.VMEM((_DEC_ROWS, LANES), F32), pltpu.VMEM((_DEC_ROWS, LANES), F32),
                        pltpu.VMEM((_DEC_ROWS, LANES), F32)],
    )
    return pl.pallas_call(
        functools.partial(_dattn_kernel, lam_init=lam_init, n_steps=n_steps),
        grid_spec=grid_spec,
        out_shape=jax.ShapeDtypeStruct((db, _DEC_ROWS // 2, LANES), F32),
        compiler_params=_cparams(("parallel", "arbitrary")),
        name="dattn",
    )(page_table, qbd, *([cache_kt] * npg), *([cache_v2] * npg), knew_t, vnew2, lq, sg)


def _gelu_tanh(x):
    return 0.5 * x * (1.0 + jnp.tanh(math.sqrt(2.0 / math.pi) * (x + 0.044715 * (x * x * x))))


def _softplus(x):
    return jnp.maximum(x, 0.0) + jnp.log(1.0 + jnp.exp(-jnp.abs(x)))


def _lru_block(x, shifted, gr, h_in, seg_row, seg_len, cw, cb, wax, ba, bx, lam):
    xc = cb + (((shifted[2] * cw[0:1] + shifted[1] * cw[1:2]) + shifted[0] * cw[2:3]) + x * cw[3:4])
    gates = jnp.dot(xc.astype(BF16), wax, preferred_element_type=F32)
    r = _sigmoid(gates[:, :LRU_BLOCK] + ba)
    i = _sigmoid(gates[:, LRU_BLOCK:] + bx)
    log_a = (-LRU_C * _softplus(-lam)) * r
    a = jnp.exp(log_a)
    b = jnp.sqrt(1.0 - jnp.exp(2.0 * log_a)) * (i * xc)
    d = 1
    while d < seg_len:
        if d < SUBLANES:
            keep = seg_row >= d
            a_sh = pltpu.roll(a, d, 0)
            b_sh = pltpu.roll(b, d, 0)
            b = jnp.where(keep, a * b_sh + b, b)
            a = jnp.where(keep, a * a_sh, a)
        else:
            a_sh = jnp.concatenate([jnp.ones((d, LANES), F32), a[:-d]], axis=0)
            b_sh = jnp.concatenate([jnp.zeros((d, LANES), F32), b[:-d]], axis=0)
            b = a * b_sh + b
            a = a * a_sh
        d *= 2
    h = a * h_in + b
    return h, h * _gelu_tanh(gr)


def _lru_sample_kernel(xr_ref, gr_ref, p1_ref, p2_ref, p3_ref, h0_ref, cw_ref, cb_ref, wax_ref,
                       ba_ref, bx_ref, lam_ref, y_ref, h_ref, *, seg_len):
    tm = xr_ref.shape[0]
    seg_row = lax.broadcasted_iota(jnp.int32, (tm, LANES), 0) & (seg_len - 1)
    prevs = (p1_ref, p2_ref, p3_ref)
    for n in range(N_LRU_BLOCKS):
        cs = slice(n * LANES, (n + 1) * LANES)
        x = xr_ref[:, cs]
        shifted = [jnp.where(seg_row < j, prevs[j - 1][:, cs], pltpu.roll(x, j, 0))
                   for j in range(1, CONV_W)]
        h, y = _lru_block(x, shifted, gr_ref[:, cs], h0_ref[:, cs], seg_row, seg_len,
                          cw_ref[:, cs], cb_ref[:, cs], wax_ref[n], ba_ref[:, cs], bx_ref[:, cs],
                          lam_ref[:, cs])
        y_ref[:, cs] = y.astype(BF16)
        h_ref[:, cs] = h


def _lru_sample(xr, gr, prevs, h0, cw, cb, wax, ba, bx, lam, seg_len):
    n = xr.shape[0]
    full = lambda shape: pl.BlockSpec(shape, lambda i: (0,) * len(shape))
    return pl.pallas_call(
        functools.partial(_lru_sample_kernel, seg_len=seg_len),
        grid=(1,),
        in_specs=[full((n, D_RNN))] * 6 + [full((CONV_W, D_RNN)), full((1, D_RNN)),
                                            full((N_LRU_BLOCKS, LRU_BLOCK, 2 * LRU_BLOCK)),
                                            full((1, D_RNN)), full((1, D_RNN)), full((1, D_RNN))],
        out_specs=[full((n, D_RNN)), full((n, D_RNN))],
        out_shape=[jax.ShapeDtypeStruct((n, D_RNN), BF16), jax.ShapeDtypeStruct((n, D_RNN), F32)],
        compiler_params=_cparams(("arbitrary",)),
        name="lru_sample",
    )(xr, gr, *prevs, h0, cw, cb, wax, ba, bx, lam)


def _merge_kernel(o_ref, *refs, tiles_per_seq):
    if tiles_per_seq is None:
        y_ref, gt_ref, x_ref, woa_ref, wol_ref, wout_ref, out_ref = refs
        o = o_ref[...]
    else:
        ol_ref, y_ref, gt_ref, x_ref, woa_ref, wol_ref, wout_ref, out_ref = refs
        is_last = pl.program_id(0) % tiles_per_seq == tiles_per_seq - 1
        o = jnp.where(is_last, ol_ref[...], o_ref[...])
    ya = jnp.dot(o, woa_ref[...], preferred_element_type=F32)
    yl = jnp.dot(y_ref[...], wol_ref[...], preferred_element_type=F32)
    merged = _sigmoid(gt_ref[:, :D_MODEL]) * ya + _sigmoid(gt_ref[:, D_MODEL:]) * yl
    out_ref[...] = x_ref[...] + jnp.dot(merged.astype(BF16), wout_ref[...],
                                        preferred_element_type=F32)


def _merge(o, y, gt, x, woa, wol, wout, tiles_per_seq=None):
    n = x.shape[0]
    tm = TOK_TILE
    row = lambda width: pl.BlockSpec((tm, width), lambda i: (i, 0))
    if tiles_per_seq is None:
        o_specs, o_args = [row(Q_W)], [o]
    else:
        nt = tiles_per_seq
        o_specs = [pl.BlockSpec((tm, Q_W),
                                lambda i: ((i // nt) * (nt - 1) + jnp.minimum(i % nt, nt - 2), 0)),
                   pl.BlockSpec((tm, Q_W), lambda i: (i // nt, 0))]
        o_args = list(o)
    return pl.pallas_call(
        functools.partial(_merge_kernel, tiles_per_seq=tiles_per_seq),
        grid=(n // tm,),
        in_specs=o_specs + [row(D_RNN), row(2 * D_MODEL), row(D_MODEL),
                            _resident((Q_W, D_MODEL)), _resident((D_RNN, D_MODEL)),
                            _resident((D_MODEL, D_MODEL))],
        out_specs=row(D_MODEL),
        out_shape=jax.ShapeDtypeStruct((n, D_MODEL), F32),
        compiler_params=_cparams(("parallel",)),
        name="merge",
    )(*o_args, y, gt, x, woa, wol, wout)


_FF_CHUNK = 256


def _ffn_kernel(x_ref, g_ref, wgu_ref, wd_ref, out_ref, act_scr):
    x = x_ref[...]
    h = _rmsnorm_rows(x, g_ref[...]).astype(BF16)
    for c0 in range(0, D_FF, _FF_CHUNK):
        gate = jnp.dot(h, wgu_ref[:, c0:c0 + _FF_CHUNK], preferred_element_type=F32)
        up = jnp.dot(h, wgu_ref[:, D_FF + c0:D_FF + c0 + _FF_CHUNK], preferred_element_type=F32)
        act_scr[:, c0:c0 + _FF_CHUNK] = (gate * _sigmoid(gate) * up).astype(BF16)
    out_ref[...] = x + jnp.dot(act_scr[...], wd_ref[...], preferred_element_type=F32)


def _ffn(x, g, wgu, wd):
    n = x.shape[0]
    tm = TOK_TILE
    row = pl.BlockSpec((tm, D_MODEL), lambda i: (i, 0))
    return pl.pallas_call(
        _ffn_kernel,
        grid=(n // tm,),
        in_specs=[row, _resident((1, D_MODEL)), _resident((D_MODEL, 2 * D_FF)),
                  _resident((D_FF, D_MODEL))],
        out_specs=row,
        out_shape=jax.ShapeDtypeStruct((n, D_MODEL), F32),
        scratch_shapes=[pltpu.VMEM((tm, D_FF), BF16)],
        compiler_params=_cparams(("parallel",)),
        name="ffn",
    )(x, g, wgu, wd)


def _rope_tables(pos):
    inv_freq = 1.0 / (ROPE_THETA ** (jnp.arange(0, HEAD_DIM, 2, dtype=F32) / HEAD_DIM))
    ang = pos.astype(F32)[:, None] * inv_freq[None, :]
    cos = jnp.cos(ang)
    sin = jnp.sin(ang)
    return jnp.tile(cos, (1, 4)), jnp.tile(jnp.concatenate([-sin, sin], axis=-1), (1, 2))


def kernel(x_prompt, x_sample, cache_k, cache_v, state_h, state_conv, page_table, meta_tokens,
           norm1_g, w_in, q_norm_g, k_norm_g, lambda_qk, subln_g, w_o_attn,
           conv_w, conv_b, lru_w_a, lru_b_a, lru_w_x, lru_b_x, lru_lambda, w_o_lru,
           w_out, norm2_g, w_gu, w_down):
    n_batch, seq, _ = x_prompt.shape
    db, dec_seq, _ = x_sample.shape
    depth = w_in.shape[0]
    n_pages = page_table.shape[1]
    page = cache_k.shape[2]
    past_len = n_pages * page
    t_real = seq + N_META
    t_pad = -(-t_real // ATT_BLOCK) * ATT_BLOCK
    assert ATT_BLOCK == TOK_TILE and dec_seq == SUBLANES and (db * dec_seq) % TOK_TILE == 0
    assert page == LANES and n_pages % PAGES_PER_STEP == 0

    xp = jnp.pad(x_prompt, ((0, 0), (N_META, t_pad - t_real), (0, 0)))
    xp = lax.dynamic_update_slice(
        xp, jnp.broadcast_to(meta_tokens.astype(F32)[None], (n_batch, N_META, D_MODEL)), (0, 0, 0))
    xp = xp.reshape(n_batch * t_pad, D_MODEL)
    xs = x_sample.reshape(db * dec_seq, D_MODEL)

    cos_p, sin_p = _rope_tables(jnp.arange(t_pad, dtype=jnp.int32))
    pos_s = past_len + (jnp.arange(db * dec_seq, dtype=jnp.int32) % dec_seq)
    cos_s, sin_s = _rope_tables(pos_s)

    ckt = jnp.transpose(cache_k, (0, 1, 3, 4, 5, 2)).reshape(cache_k.shape[0], cache_k.shape[1], KV_W, page)
    cv2 = cache_v.reshape(cache_v.shape[0], cache_v.shape[1], page * N_KV_HEADS, 2 * HEAD_DIM)
    eye_h = jnp.eye(N_KV_HEADS, dtype=BF16)
    eye_c = jnp.eye(2, dtype=BF16)

    outs = {k: [] for k in ("hp", "cp", "ks", "vs", "hs", "cs")}
    kv_all = None
    for l in range(depth):
        lam_init = 0.8 - 0.6 * math.exp(-0.3 * l)
        g1 = norm1_g[l][None]
        win = w_in[l].astype(BF16)
        gq = jnp.tile(q_norm_g[l], 2)[None]
        gk = jnp.tile(k_norm_g[l], 2)[None]
        lq = lambda_qk[l]
        sg = subln_g[l][None]
        woa = w_o_attn[l].astype(BF16)
        cw = conv_w[l]
        cb = conv_b[l][None]
        wax = jnp.concatenate([lru_w_a[l], lru_w_x[l]], axis=-1).astype(BF16)
        ba = lru_b_a[l][None]
        bx = lru_b_x[l][None]
        lam_l = lru_lambda[l][None]
        wol = w_o_lru[l].astype(BF16)
        wout = w_out[l].astype(BF16)
        g2 = norm2_g[l][None]
        wgu = w_gu[l].astype(BF16)
        wd = w_down[l].astype(BF16)

        q, k_all, kb, v_all, vb, gt, y, hl, cvn = _inproj_prompt(
            xp, g1, win, cos_p, sin_p, gq, gk, cw, cb, wax, ba, bx, lam_l, n_batch, t_pad, t_real,
            l, depth, kv_all)
        kv_all = (k_all, v_all)
        o = _pattn(q, kb, vb, lq, sg, n_batch, t_pad, t_real, lam_init)
        xp = _ffn(_merge(o, y, gt, xp, woa, wol, wout, t_pad // TOK_TILE), g2, wgu, wd)
        outs["hp"].append(hl[:, SUBLANES - 1])
        outs["cp"].append(cvn[:, SUBLANES - (CONV_W - 1):])

        q, k, v, xr, gr, gt = _inproj_sample(xs, g1, win, cos_s, sin_s, gq, gk)
        q6 = q.reshape(db, dec_seq, N_KV_HEADS, GROUP, 2, HEAD_DIM)
        qr = jnp.transpose(q6, (0, 2, 4, 3, 1, 5)).reshape(db, N_KV_HEADS, 2, GROUP * dec_seq, HEAD_DIM)
        qbd = jnp.einsum("bhcrd,hx,cy->bhcrxyd", qr, eye_h, eye_c).reshape(db, _DEC_ROWS, KV_W)
        knew_t = jnp.pad(jnp.transpose(k.reshape(db, dec_seq, KV_W), (0, 2, 1)),
                         ((0, 0), (0, 0), (0, page - dec_seq)))
        vnew2 = jnp.pad(v.reshape(db, dec_seq * N_KV_HEADS, 2 * HEAD_DIM),
                        ((0, 0), (0, (page - dec_seq) * N_KV_HEADS), (0, 0)))
        od = _dattn(page_table, qbd, ckt, cv2, knew_t, vnew2, lq, sg, l, lam_init)
        o = jnp.transpose(od.reshape(db, N_KV_HEADS, GROUP, dec_seq, 2 * HEAD_DIM),
                          (0, 3, 1, 2, 4)).reshape(db * dec_seq, Q_W).astype(BF16)
        conv_ext = jnp.concatenate([state_conv[l], jnp.zeros((db, dec_seq, D_RNN), F32)], axis=1)
        prevs = [conv_ext[:, CONV_W - 1 - j:CONV_W - 1 - j + dec_seq].reshape(db * dec_seq, D_RNN)
                 for j in range(1, CONV_W)]
        h0 = jnp.repeat(state_h[l], dec_seq, axis=0)
        y, hs = _lru_sample(xr, gr, prevs, h0, cw, cb, wax, ba, bx, lam_l, dec_seq)
        xs = _ffn(_merge(o, y, gt, xs, woa, wol, wout), g2, wgu, wd)
        outs["ks"].append(k.reshape(db, dec_seq, N_KV_HEADS, 2, HEAD_DIM))
        outs["vs"].append(v.reshape(db, dec_seq, N_KV_HEADS, 2 * HEAD_DIM))
        outs["hs"].append(hs.reshape(db, dec_seq, D_RNN)[:, -1])
        outs["cs"].append(xr.reshape(db, dec_seq, D_RNN)[:, dec_seq - (CONV_W - 1):])

    y_prompt = xp.reshape(n_batch, t_pad, D_MODEL)[:, N_META:t_real]
    y_sample = xs.reshape(db, dec_seq, D_MODEL)
    k_prompt = jnp.transpose(kv_all[0].reshape(depth, n_batch, N_KV_HEADS, 2, HEAD_DIM, t_real),
                             (0, 1, 5, 2, 3, 4))
    v_prompt = kv_all[1].reshape(depth, n_batch, t_real, N_KV_HEADS, 2 * HEAD_DIM)
    return (y_prompt, y_sample, k_prompt, v_prompt, jnp.stack(outs["hp"]), jnp.stack(outs["cp"]),
            jnp.stack(outs["ks"]), jnp.stack(outs["vs"]), jnp.stack(outs["hs"]), jnp.stack(outs["cs"]))
```

```python
import functools
import math

import jax
import jax.numpy as jnp
from jax import lax
from jax.experimental import pallas as pl
from jax.experimental.pallas import tpu as pltpu

F32 = jnp.float32
BF16 = jnp.bfloat16

D_MODEL = 1024
N_META = 16
N_KV_HEADS = 4
GROUP = 2
HEAD_DIM = 64
Q_W = 1024
KV_W = 512
D_RNN = 1280
N_LRU_BLOCKS = 10
LRU_BLOCK = 128
CONV_W = 4
LRU_C = 8.0
D_FF = 2816
N_IN = Q_W + 2 * KV_W + 2 * D_RNN + 2 * D_MODEL
ROPE_THETA = 10000.0
NORM_EPS = 1e-6

LANES = 128
SUBLANES = 8
TOK_TILE = 256
ATT_BLOCK = 256
PAGES_PER_STEP = 32
LAST_ROW_IN_TILE = 15
Q_TILE_BLOCKS = 2
KEY_BLOCKS_PER_CHUNK = 4
VMEM_LIMIT = 48 * 1024 * 1024


def _cparams(sem):
    return pltpu.CompilerParams(dimension_semantics=sem, vmem_limit_bytes=VMEM_LIMIT)


def _resident(shape):
    return pl.BlockSpec(shape, lambda *_: (0,) * len(shape), pipeline_mode=pl.Buffered(1))


def _rmsnorm_rows(x, g):
    ms = jnp.mean(x * x, axis=-1, keepdims=True)
    return x * lax.rsqrt(ms + NORM_EPS) * g


def _sigmoid(x):
    return 1.0 / (1.0 + jnp.exp(-x))


_V_START = Q_W + KV_W
_REST_SPLITS = ((Q_W + 2 * KV_W, D_RNN), (Q_W + 2 * KV_W + D_RNN, D_RNN),
                (Q_W + 2 * KV_W + 2 * D_RNN, 2 * D_MODEL))
_N_CHUNK = 512


def _norm_rope(x, cos, sin, g, low, first_half):
    ss = x * x
    s_low = jnp.sum(jnp.where(low, ss, 0.0), axis=-1, keepdims=True)
    s_high = jnp.sum(jnp.where(low, 0.0, ss), axis=-1, keepdims=True)
    ms = jnp.where(low, s_low, s_high) * (1.0 / HEAD_DIM)
    y = x * lax.rsqrt(ms + NORM_EPS) * g
    partner = jnp.where(first_half, pltpu.roll(y, LANES - HEAD_DIM // 2, 1),
                        pltpu.roll(y, HEAD_DIM // 2, 1))
    return y * cos + partner * sin


def _inproj_kernel(x_ref, g_ref, w_ref, cos_ref, sin_ref, gq_ref, gk_ref, *refs, prompt, tiles_per_seq,
                   n_alias=0):
    if prompt:
        refs = refs[:6] + refs[6 + n_alias:]
        (cw_ref, cb_ref, wax_ref, ba_ref, bx_ref, lam_ref,
         q_ref, k_ref, kb_ref, v_ref, vb_ref, gt_ref, y_ref, hl_ref, cv_ref,
         xr_ref, gr_ref, xcar, hcar) = refs
    else:
        q_ref, k_ref, v_ref, xr_ref, gr_ref, gt_ref = refs
        kb_ref = vb_ref = None
    tm = x_ref.shape[0]
    h = _rmsnorm_rows(x_ref[...], g_ref[...]).astype(BF16)

    def project(ref, start, width, c0):
        cw = min(_N_CHUNK, width - c0)
        ref[:, c0:c0 + cw] = jnp.dot(h, w_ref[:, start + c0:start + c0 + cw],
                                     preferred_element_type=F32)

    for ref, (start, width) in zip((xr_ref, gr_ref), _REST_SPLITS[:2]):
        for c0 in range(0, width, _N_CHUNK):
            project(ref, start, width, c0)

    lane = lax.broadcasted_iota(jnp.int32, (tm, LANES), 1)
    low = lane < HEAD_DIM
    first_half = (lane & (HEAD_DIM - 1)) < HEAD_DIM // 2
    n_q = Q_W // LANES

    def qk_chunk(c0):
        z = jnp.dot(h, w_ref[:, c0:c0 + _N_CHUNK], preferred_element_type=F32)
        for jb in range(_N_CHUNK // LANES):
            j = c0 // LANES + jb
            out = _norm_rope(z[:, jb * LANES:(jb + 1) * LANES], cos_ref[...], sin_ref[...],
                             gq_ref[...] if j < n_q else gk_ref[...], low, first_half)
            if j < n_q:
                q_ref[:, j * LANES:(j + 1) * LANES] = (out * (HEAD_DIM ** -0.5)).astype(BF16)
            else:
                cs = slice((j - n_q) * LANES, (j - n_q + 1) * LANES)
                if prompt:
                    k_ref[cs, :] = out.T
                    kb_ref[:, cs] = out.astype(BF16)
                else:
                    k_ref[:, cs] = out

    def v_chunk():
        z = jnp.dot(h, w_ref[:, _V_START:_V_START + KV_W], preferred_element_type=F32)
        if prompt:
            for hh in range(N_KV_HEADS):
                v_ref[pl.ds(hh, tm, stride=N_KV_HEADS), :] = z[:, hh * LANES:(hh + 1) * LANES]
            vb_ref[...] = z.astype(BF16)
        else:
            v_ref[...] = z

    gt_start, gt_width = _REST_SPLITS[2]
    work = [functools.partial(qk_chunk, c0) for c0 in range(0, Q_W + KV_W, _N_CHUNK)] + [v_chunk]
    work += [functools.partial(project, gt_ref, gt_start, gt_width, c0)
             for c0 in range(0, gt_width, _N_CHUNK)]
    if not prompt:
        for item in work:
            item()
        return

    first_tile = pl.program_id(0) % tiles_per_seq == 0
    row = lax.broadcasted_iota(jnp.int32, (tm, LANES), 0)
    row8 = lax.broadcasted_iota(jnp.int32, (SUBLANES, LANES), 0)
    for n in range(N_LRU_BLOCKS):
        cs = slice(n * LANES, (n + 1) * LANES)
        x = xr_ref[:, cs]
        car = jnp.where(first_tile, 0.0, xcar[:, cs])
        h_in = jnp.where(first_tile, 0.0, hcar[0:1, cs])
        shifted = []
        for j in range(1, CONV_W):
            main = pltpu.roll(x, j, 0)
            head = jnp.where(row8 < j, pltpu.roll(car, j, 0), main[0:SUBLANES])
            shifted.append(jnp.concatenate([head, main[SUBLANES:]], axis=0))
        hs, y = _lru_block(x, shifted, gr_ref[:, cs], h_in, row, tm,
                           cw_ref[:, cs], cb_ref[:, cs], wax_ref[n], ba_ref[:, cs], bx_ref[:, cs],
                           lam_ref[:, cs])
        y_ref[:, cs] = y.astype(BF16)
        xcar[:, cs] = x[tm - SUBLANES:tm]
        hcar[0:1, cs] = hs[tm - 1:tm]
        r0 = LAST_ROW_IN_TILE - (SUBLANES - 1)
        hl_ref[:, cs] = hs[r0:r0 + SUBLANES]
        cv_ref[:, cs] = x[r0:r0 + SUBLANES]
        if n < len(work):
            work[n]()
    for item in work[N_LRU_BLOCKS:]:
        item()


def _inproj_prompt(x, g, w, cos, sin, gq, gk, cw, cb, wax, ba, bx, lam, n_batch, t_pad, t_real,
                   layer, depth, kv_all):
    n = x.shape[0]
    tm = TOK_TILE
    nt = t_pad // tm
    last_tile, last_row = divmod(t_real - 1, tm)
    assert last_tile == nt - 1 and last_row == LAST_ROW_IN_TILE
    row = lambda width: pl.BlockSpec((tm, width), lambda i: (i, 0))
    tab = pl.BlockSpec((tm, LANES), lambda i: (i % nt, 0))
    tail = pl.BlockSpec((None, SUBLANES, D_RNN), lambda i: (i // nt, 0, 0))
    outs = [(Q_W, BF16), None, (KV_W, BF16), None, (KV_W, BF16), (2 * D_MODEL, F32), (D_RNN, BF16)]
    specs = [row(w_[0]) if w_ else None for w_ in outs]
    shapes = [jax.ShapeDtypeStruct((n, w_[0]), w_[1]) if w_ else None for w_ in outs]
    specs[1] = pl.BlockSpec((None, None, KV_W, tm), lambda i: (layer, i // nt, 0, i % nt))
    shapes[1] = jax.ShapeDtypeStruct((depth, n_batch, KV_W, t_real), F32)
    specs[3] = pl.BlockSpec((None, None, tm * N_KV_HEADS, LANES), lambda i: (layer, i // nt, i % nt, 0))
    shapes[3] = jax.ShapeDtypeStruct((depth, n_batch, t_real * N_KV_HEADS, LANES), F32)
    n_fixed = 13
    alias_in = [] if kv_all is None else list(kv_all)
    aliases = {n_fixed + j: o for j, o in enumerate((1, 3)[:len(alias_in)])}
    return pl.pallas_call(
        functools.partial(_inproj_kernel, prompt=True, tiles_per_seq=nt, n_alias=len(alias_in)),
        input_output_aliases=aliases,
        grid=(n // tm,),
        in_specs=[row(D_MODEL), _resident((1, D_MODEL)), _resident((D_MODEL, N_IN)), tab, tab,
                  _resident((1, LANES)), _resident((1, LANES)),
                  _resident((CONV_W, D_RNN)), _resident((1, D_RNN)),
                  _resident((N_LRU_BLOCKS, LRU_BLOCK, 2 * LRU_BLOCK)),
                  _resident((1, D_RNN)), _resident((1, D_RNN)), _resident((1, D_RNN))]
        + [pl.BlockSpec(memory_space=pl.ANY)] * len(alias_in),
        out_specs=specs + [tail, tail],
        out_shape=shapes + [jax.ShapeDtypeStruct((n_batch, SUBLANES, D_RNN), F32)] * 2,
        scratch_shapes=[pltpu.VMEM((tm, D_RNN), F32), pltpu.VMEM((tm, D_RNN), F32),
                        pltpu.VMEM((SUBLANES, D_RNN), F32), pltpu.VMEM((SUBLANES, D_RNN), F32)],
        compiler_params=_cparams(("arbitrary",)),
        name="inproj_prompt",
    )(x, g, w, cos, sin, gq, gk, cw, cb, wax, ba, bx, lam, *alias_in)


def _inproj_sample(x, g, w, cos, sin, gq, gk):
    n = x.shape[0]
    tm = TOK_TILE
    row = lambda width: pl.BlockSpec((tm, width), lambda i: (i, 0))
    tab = row(LANES)
    outs = [(Q_W, BF16), (KV_W, F32), (KV_W, F32), (D_RNN, F32), (D_RNN, F32), (2 * D_MODEL, F32)]
    return pl.pallas_call(
        functools.partial(_inproj_kernel, prompt=False, tiles_per_seq=None),
        grid=(n // tm,),
        in_specs=[row(D_MODEL), _resident((1, D_MODEL)), _resident((D_MODEL, N_IN)), tab, tab,
                  _resident((1, LANES)), _resident((1, LANES))],
        out_specs=[row(w_) for w_, _ in outs],
        out_shape=[jax.ShapeDtypeStruct((n, w_), dt) for w_, dt in outs],
        compiler_params=_cparams(("parallel",)),
        name="inproj_sample",
    )(x, g, w, cos, sin, gq, gk)


def _lambda_full(lq, lam_init):
    a = jnp.sum(lq[0:1, :] * lq[1:2, :], axis=-1, keepdims=True)
    b = jnp.sum(lq[2:3, :] * lq[3:4, :], axis=-1, keepdims=True)
    return jnp.exp(a) - jnp.exp(b) + lam_init


def _subln(o, g, lam_init):
    return _rmsnorm_rows(o, g) * (1.0 - lam_init)


def _pattn_kernel(q_ref, kb_ref, vb_ref, lq_ref, sg_ref, o_ref, qs_scr, m_scr, l_scr, acc_scr,
                  *, lam_init, tq, q_block):
    tb = ATT_BLOCK
    rows = 2 * GROUP * tq
    q_blocks = max(tq // tb, 1)
    qi = pl.program_id(2) * q_blocks if q_block is None else q_block
    lane = lax.broadcasted_iota(jnp.int32, (tq, LANES), 1)
    low = lane < HEAD_DIM
    for c in range(2):
        for g in range(GROUP):
            qg = q_ref[:, g * LANES:(g + 1) * LANES].astype(F32)
            r0 = (c * GROUP + g) * tq
            sel = jnp.where(low, qg, 0.0) if c == 0 else jnp.where(low, 0.0, qg)
            qs_scr[r0:r0 + tq, :] = sel.astype(BF16)
    m_scr[...] = jnp.full(m_scr.shape, -jnp.inf, F32)
    l_scr[...] = jnp.zeros(l_scr.shape, F32)
    acc_scr[...] = jnp.zeros(acc_scr.shape, F32)

    def update(start, n_keys, first_visible):
        kc = kb_ref[pl.ds(start, n_keys), :]
        vc = vb_ref[pl.ds(start, n_keys), :]
        s = lax.dot_general(qs_scr[...], kc, (((1,), (1,)), ((), ())),
                            preferred_element_type=F32)
        if first_visible is not None:
            t = lax.broadcasted_iota(jnp.int32, s.shape, 0) & (tq - 1)
            col = lax.broadcasted_iota(jnp.int32, s.shape, 1)
            s = jnp.where(col <= t + first_visible, s, -jnp.inf)
        m_prev = m_scr[...]
        m_new = jnp.maximum(m_prev, jnp.max(s, axis=1, keepdims=True))
        alpha = jnp.exp(m_prev - m_new)
        p = jnp.exp(s - jnp.concatenate([m_new] * (n_keys // LANES), axis=1))
        l_scr[...] = alpha * l_scr[...] + jnp.sum(p, axis=1, keepdims=True)
        acc_scr[...] = alpha * acc_scr[...] + jnp.dot(p.astype(BF16), vc,
                                                      preferred_element_type=F32)
        m_scr[...] = m_new

    nb = KEY_BLOCKS_PER_CHUNK

    def body(j, carry):
        update(pl.multiple_of(j * nb * tb, nb * tb), nb * tb, None)
        return carry

    lax.fori_loop(0, qi // nb, body, 0)

    if q_block is None:
        for r in range(0, nb, q_blocks):
            @pl.when(qi % nb == r)
            def _(r=r):
                update(pl.multiple_of((qi - r) * tb, tb), (r + q_blocks) * tb, r * tb)
    else:
        r = q_block % nb
        update((q_block - r) * tb, (r + q_blocks) * tb, r * tb)

    on = acc_scr[...] / l_scr[...]
    lam = _lambda_full(lq_ref[...], lam_init)
    od = on[0:rows // 2] - lam * on[rows // 2:rows]
    res = _subln(od, sg_ref[...], lam_init).astype(BF16)
    if tq < o_ref.shape[0]:
        o_ref[...] = jnp.zeros(o_ref.shape, BF16)
    for g in range(GROUP):
        o_ref[0:tq, g * LANES:(g + 1) * LANES] = res[g * tq:(g + 1) * tq]


def _pattn(q, kb, vb, lq, sg, n_batch, t_pad, t_real, lam_init):
    tb = ATT_BLOCK
    nq = t_pad // tb
    n_last = t_real - (nq - 1) * tb
    assert 0 < n_last <= tb and n_last % (2 * SUBLANES) == 0
    kv = pl.BlockSpec((t_pad, LANES), lambda b, h, *i: (b, h))

    q3 = q.reshape(n_batch, t_pad, Q_W)

    def call(tq, q_block, grid, q_map, o_map, o_rows, name):
        rows = 2 * GROUP * tq
        return pl.pallas_call(
            functools.partial(_pattn_kernel, lam_init=lam_init, tq=tq, q_block=q_block),
            grid=grid,
            in_specs=[pl.BlockSpec((None, tq, GROUP * LANES), q_map), kv, kv,
                      _resident((4, HEAD_DIM)), _resident((1, LANES))],
            out_specs=pl.BlockSpec((max(tq, tb), GROUP * LANES), o_map),
            out_shape=jax.ShapeDtypeStruct((o_rows, Q_W), BF16),
            scratch_shapes=[pltpu.VMEM((rows, LANES), BF16),
                            pltpu.VMEM((rows, LANES), F32),
                            pltpu.VMEM((rows, LANES), F32),
                            pltpu.VMEM((rows, LANES), F32)],
            compiler_params=_cparams(("parallel",) * (len(grid) - 1) + ("arbitrary",)),
            name=name,
        )(q3, kb, vb, lq, sg)

    tq = Q_TILE_BLOCKS * tb
    n_main = (nq - 1) * tb // tq
    assert n_main * tq == (nq - 1) * tb and KEY_BLOCKS_PER_CHUNK % Q_TILE_BLOCKS == 0
    o_main = call(tq, None, (n_batch, N_KV_HEADS, n_main),
                  lambda b, h, i: (b, i, h), lambda b, h, i: (b * n_main + i, h),
                  n_batch * (nq - 1) * tb, "pattn")
    o_last = call(n_last, nq - 1, (n_batch, N_KV_HEADS),
                  lambda b, h: (b, (nq - 1) * tb // n_last, h), lambda b, h: (b, h),
                  n_batch * tb, "pattn_last")
    return o_main, o_last


_DEC_ROWS = N_KV_HEADS * 2 * GROUP * 8
_ROWS_H = _DEC_ROWS // N_KV_HEADS


def _dattn_kernel(pt_ref, qb_ref, *refs, lam_init, n_steps):
    npg = PAGES_PER_STEP
    k_refs = refs[:npg]
    v_refs = refs[npg:2 * npg]
    kn_ref, vn_ref, lq_ref, sg_ref, o_ref, m_scr, l_scr, acc_scr = refs[2 * npg:]
    pc = pl.program_id(1)

    @pl.when(pc == 0)
    def _():
        m_scr[...] = jnp.full(m_scr.shape, -jnp.inf, F32)
        l_scr[...] = jnp.zeros(l_scr.shape, F32)
        acc_scr[...] = jnp.zeros(acc_scr.shape, F32)

    qb = qb_ref[...]

    def update(pages_k, pages_v, masked):
        ss = []
        for kp in pages_k:
            s = jnp.dot(qb, kp[...].astype(BF16), preferred_element_type=F32)
            if masked:
                row = lax.broadcasted_iota(jnp.int32, s.shape, 0)
                col = lax.broadcasted_iota(jnp.int32, s.shape, 1)
                s = jnp.where(col <= (row & 7), s, -jnp.inf)
            ss.append(s)
        m_prev = m_scr[...]
        m_cur = ss[0]
        for s in ss[1:]:
            m_cur = jnp.maximum(m_cur, s)
        m_new = jnp.maximum(m_prev, jnp.max(m_cur, axis=1, keepdims=True))
        alpha = jnp.exp(m_prev - m_new)
        l_new = alpha * l_scr[...]
        pv = [None] * N_KV_HEADS
        n_keys = pages_v[0].shape[0] // N_KV_HEADS
        for s, vp in zip(ss, pages_v):
            p = jnp.exp(s - m_new)
            l_new = l_new + jnp.sum(p, axis=1, keepdims=True)
            pb = p.astype(BF16)
            for h in range(N_KV_HEADS):
                vh = vp[pl.ds(h, n_keys, stride=N_KV_HEADS), :].astype(BF16)
                d = jnp.dot(pb[h * _ROWS_H:(h + 1) * _ROWS_H], vh, preferred_element_type=F32)
                pv[h] = d if pv[h] is None else pv[h] + d
        l_scr[...] = l_new
        acc_scr[...] = alpha * acc_scr[...] + jnp.concatenate(pv, axis=0)
        m_scr[...] = m_new

    update(k_refs, v_refs, False)

    @pl.when(pc == n_steps - 1)
    def _():
        update([kn_ref], [vn_ref], True)
        on = acc_scr[...] / l_scr[...]
        lam = _lambda_full(lq_ref[...], lam_init)
        for h in range(N_KV_HEADS):
            blk = on[h * _ROWS_H:(h + 1) * _ROWS_H]
            od = blk[0:_ROWS_H // 2] - lam * blk[_ROWS_H // 2:_ROWS_H]
            o_ref[h * (_ROWS_H // 2):(h + 1) * (_ROWS_H // 2), :] = _subln(od, sg_ref[...], lam_init)


def _dattn(page_table, qbd, cache_kt, cache_v2, knew_t, vnew2, lq, sg, layer, lam_init):
    db, n_pages = page_table.shape
    npg = PAGES_PER_STEP
    n_steps = n_pages // npg
    rows, page = cache_kt.shape[2], cache_kt.shape[3]

    def page_spec(i):
        return pl.BlockSpec((None, None, rows, page),
                            lambda b, pc, pt: (layer, pt[b, pc * npg + i], 0, 0))

    per_b = lambda r, w: pl.BlockSpec((None, r, w), lambda b, pc, pt: (b, 0, 0))
    const = lambda shape: pl.BlockSpec(shape, lambda b, pc, pt: (0,) * len(shape))
    grid_spec = pltpu.PrefetchScalarGridSpec(
        num_scalar_prefetch=1,
        grid=(db, n_steps),
        in_specs=[per_b(_DEC_ROWS, rows)] + [page_spec(i) for i in range(npg)]
        + [page_spec(i) for i in range(npg)]
        + [per_b(rows, page), per_b(rows, page), const((4, HEAD_DIM)), const((1, LANES))],
        out_specs=per_b(_DEC_ROWS // 2, LANES),
        scratch_shapes=[pltpu.VMEM((_DEC_ROWS, LANES), F32), pltpu.VMEM((_DEC_ROWS, LANES), F32),
                        pltpu.VMEM((_DEC_ROWS, LANES), F32)],
    )
    return pl.pallas_call(
        functools.partial(_dattn_kernel, lam_init=lam_init, n_steps=n_steps),
        grid_spec=grid_spec,
        out_shape=jax.ShapeDtypeStruct((db, _DEC_ROWS // 2, LANES), F32),
        compiler_params=_cparams(("parallel", "arbitrary")),
        name="dattn",
    )(page_table, qbd, *([cache_kt] * npg), *([cache_v2] * npg), knew_t, vnew2, lq, sg)


def _gelu_tanh(x):
    return 0.5 * x * (1.0 + jnp.tanh(math.sqrt(2.0 / math.pi) * (x + 0.044715 * (x * x * x))))


def _softplus(x):
    return jnp.maximum(x, 0.0) + jnp.log(1.0 + jnp.exp(-jnp.abs(x)))


def _lru_block(x, shifted, gr, h_in, seg_row, seg_len, cw, cb, wax, ba, bx, lam):
    xc = cb + (((shifted[2] * cw[0:1] + shifted[1] * cw[1:2]) + shifted[0] * cw[2:3]) + x * cw[3:4])
    gates = jnp.dot(xc.astype(BF16), wax, preferred_element_type=F32)
    r = _sigmoid(gates[:, :LRU_BLOCK] + ba)
    i = _sigmoid(gates[:, LRU_BLOCK:] + bx)
    log_a = (-LRU_C * _softplus(-lam)) * r
    a = jnp.exp(log_a)
    b = jnp.sqrt(1.0 - jnp.exp(2.0 * log_a)) * (i * xc)
    d = 1
    while d < seg_len:
        if d < SUBLANES:
            keep = seg_row >= d
            a_sh = pltpu.roll(a, d, 0)
            b_sh = pltpu.roll(b, d, 0)
            b = jnp.where(keep, a * b_sh + b, b)
            a = jnp.where(keep, a * a_sh, a)
        else:
            a_sh = jnp.concatenate([jnp.ones((d, LANES), F32), a[:-d]], axis=0)
            b_sh = jnp.concatenate([jnp.zeros((d, LANES), F32), b[:-d]], axis=0)
            b = a * b_sh + b
            a = a * a_sh
        d *= 2
    h = a * h_in + b
    return h, h * _gelu_tanh(gr)


def _lru_sample_kernel(xr_ref, gr_ref, p1_ref, p2_ref, p3_ref, h0_ref, cw_ref, cb_ref, wax_ref,
                       ba_ref, bx_ref, lam_ref, y_ref, h_ref, *, seg_len):
    tm = xr_ref.shape[0]
    seg_row = lax.broadcasted_iota(jnp.int32, (tm, LANES), 0) & (seg_len - 1)
    prevs = (p1_ref, p2_ref, p3_ref)
    for n in range(N_LRU_BLOCKS):
        cs = slice(n * LANES, (n + 1) * LANES)
        x = xr_ref[:, cs]
        shifted = [jnp.where(seg_row < j, prevs[j - 1][:, cs], pltpu.roll(x, j, 0))
                   for j in range(1, CONV_W)]
        h, y = _lru_block(x, shifted, gr_ref[:, cs], h0_ref[:, cs], seg_row, seg_len,
                          cw_ref[:, cs], cb_ref[:, cs], wax_ref[n], ba_ref[:, cs], bx_ref[:, cs],
                          lam_ref[:, cs])
        y_ref[:, cs] = y.astype(BF16)
        h_ref[:, cs] = h


def _lru_sample(xr, gr, prevs, h0, cw, cb, wax, ba, bx, lam, seg_len):
    n = xr.shape[0]
    full = lambda shape: pl.BlockSpec(shape, lambda i: (0,) * len(shape))
    return pl.pallas_call(
        functools.partial(_lru_sample_kernel, seg_len=seg_len),
        grid=(1,),
        in_specs=[full((n, D_RNN))] * 6 + [full((CONV_W, D_RNN)), full((1, D_RNN)),
                                            full((N_LRU_BLOCKS, LRU_BLOCK, 2 * LRU_BLOCK)),
                                            full((1, D_RNN)), full((1, D_RNN)), full((1, D_RNN))],
        out_specs=[full((n, D_RNN)), full((n, D_RNN))],
        out_shape=[jax.ShapeDtypeStruct((n, D_RNN), BF16), jax.ShapeDtypeStruct((n, D_RNN), F32)],
        compiler_params=_cparams(("arbitrary",)),
        name="lru_sample",
    )(xr, gr, *prevs, h0, cw, cb, wax, ba, bx, lam)


_FF_CHUNK = 256


def _mlp_kernel(o_ref, *refs, tiles_per_seq):
    if tiles_per_seq is None:
        (y_ref, gt_ref, x_ref, woa_ref, wol_ref, wout_ref, g_ref, wgu_ref, wd_ref,
         out_ref, act_scr) = refs
        o = o_ref[...]
    else:
        (ol_ref, y_ref, gt_ref, x_ref, woa_ref, wol_ref, wout_ref, g_ref, wgu_ref, wd_ref,
         out_ref, act_scr) = refs
        is_last = pl.program_id(0) % tiles_per_seq == tiles_per_seq - 1
        o = jnp.where(is_last, ol_ref[...], o_ref[...])
    ya = jnp.dot(o, woa_ref[...], preferred_element_type=F32)
    yl = jnp.dot(y_ref[...], wol_ref[...], preferred_element_type=F32)
    merged = _sigmoid(gt_ref[:, :D_MODEL]) * ya + _sigmoid(gt_ref[:, D_MODEL:]) * yl
    x1 = x_ref[...] + jnp.dot(merged.astype(BF16), wout_ref[...], preferred_element_type=F32)
    h = _rmsnorm_rows(x1, g_ref[...]).astype(BF16)
    for c0 in range(0, D_FF, _FF_CHUNK):
        gate = jnp.dot(h, wgu_ref[:, c0:c0 + _FF_CHUNK], preferred_element_type=F32)
        up = jnp.dot(h, wgu_ref[:, D_FF + c0:D_FF + c0 + _FF_CHUNK], preferred_element_type=F32)
        act_scr[:, c0:c0 + _FF_CHUNK] = (gate * _sigmoid(gate) * up).astype(BF16)
    out_ref[...] = x1 + jnp.dot(act_scr[...], wd_ref[...], preferred_element_type=F32)


def _mlp(o, y, gt, x, woa, wol, wout, g, wgu, wd, tiles_per_seq=None):
    n = x.shape[0]
    tm = TOK_TILE
    row = lambda width: pl.BlockSpec((tm, width), lambda i: (i, 0))
    if tiles_per_seq is None:
        o_specs, o_args = [row(Q_W)], [o]
    else:
        nt = tiles_per_seq
        o_specs = [pl.BlockSpec((tm, Q_W),
                                lambda i: ((i // nt) * (nt - 1) + jnp.minimum(i % nt, nt - 2), 0)),
                   pl.BlockSpec((tm, Q_W), lambda i: (i // nt, 0))]
        o_args = list(o)
    return pl.pallas_call(
        functools.partial(_mlp_kernel, tiles_per_seq=tiles_per_seq),
        grid=(n // tm,),
        in_specs=o_specs + [row(D_RNN), row(2 * D_MODEL), row(D_MODEL),
                            _resident((Q_W, D_MODEL)), _resident((D_RNN, D_MODEL)),
                            _resident((D_MODEL, D_MODEL)), _resident((1, D_MODEL)),
                            _resident((D_MODEL, 2 * D_FF)), _resident((D_FF, D_MODEL))],
        out_specs=row(D_MODEL),
        out_shape=jax.ShapeDtypeStruct((n, D_MODEL), F32),
        scratch_shapes=[pltpu.VMEM((tm, D_FF), BF16)],
        compiler_params=_cparams(("parallel",)),
        name="mlp",
    )(*o_args, y, gt, x, woa, wol, wout, g, wgu, wd)


def _rope_tables(pos):
    inv_freq = 1.0 / (ROPE_THETA ** (jnp.arange(0, HEAD_DIM, 2, dtype=F32) / HEAD_DIM))
    ang = pos.astype(F32)[:, None] * inv_freq[None, :]
    cos = jnp.cos(ang)
    sin = jnp.sin(ang)
    return jnp.tile(cos, (1, 4)), jnp.tile(jnp.concatenate([-sin, sin], axis=-1), (1, 2))


def kernel(x_prompt, x_sample, cache_k, cache_v, state_h, state_conv, page_table, meta_tokens,
           norm1_g, w_in, q_norm_g, k_norm_g, lambda_qk, subln_g, w_o_attn,
           conv_w, conv_b, lru_w_a, lru_b_a, lru_w_x, lru_b_x, lru_lambda, w_o_lru,
           w_out, norm2_g, w_gu, w_down):
    n_batch, seq, _ = x_prompt.shape
    db, dec_seq, _ = x_sample.shape
    depth = w_in.shape[0]
    n_pages = page_table.shape[1]
    page = cache_k.shape[2]
    past_len = n_pages * page
    t_real = seq + N_META
    t_pad = -(-t_real // ATT_BLOCK) * ATT_BLOCK
    assert ATT_BLOCK == TOK_TILE and dec_seq == SUBLANES and (db * dec_seq) % TOK_TILE == 0
    assert page == LANES and n_pages % PAGES_PER_STEP == 0

    xp = jnp.pad(x_prompt, ((0, 0), (N_META, t_pad - t_real), (0, 0)))
    xp = lax.dynamic_update_slice(
        xp, jnp.broadcast_to(meta_tokens.astype(F32)[None], (n_batch, N_META, D_MODEL)), (0, 0, 0))
    xp = xp.reshape(n_batch * t_pad, D_MODEL)
    xs = x_sample.reshape(db * dec_seq, D_MODEL)

    cos_p, sin_p = _rope_tables(jnp.arange(t_pad, dtype=jnp.int32))
    pos_s = past_len + (jnp.arange(db * dec_seq, dtype=jnp.int32) % dec_seq)
    cos_s, sin_s = _rope_tables(pos_s)

    ckt = jnp.transpose(cache_k, (0, 1, 3, 4, 5, 2)).reshape(cache_k.shape[0], cache_k.shape[1], KV_W, page)
    cv2 = cache_v.reshape(cache_v.shape[0], cache_v.shape[1], page * N_KV_HEADS, 2 * HEAD_DIM)
    eye_h = jnp.eye(N_KV_HEADS, dtype=BF16)
    eye_c = jnp.eye(2, dtype=BF16)

    outs = {k: [] for k in ("hp", "cp", "ks", "vs", "hs", "cs")}
    kv_all = None
    for l in range(depth):
        lam_init = 0.8 - 0.6 * math.exp(-0.3 * l)
        g1 = norm1_g[l][None]
        win = w_in[l].astype(BF16)
        gq = jnp.tile(q_norm_g[l], 2)[None]
        gk = jnp.tile(k_norm_g[l], 2)[None]
        lq = lambda_qk[l]
        sg = subln_g[l][None]
        woa = w_o_attn[l].astype(BF16)
        cw = conv_w[l]
        cb = conv_b[l][None]
        wax = jnp.concatenate([lru_w_a[l], lru_w_x[l]], axis=-1).astype(BF16)
        ba = lru_b_a[l][None]
        bx = lru_b_x[l][None]
        lam_l = lru_lambda[l][None]
        wol = w_o_lru[l].astype(BF16)
        wout = w_out[l].astype(BF16)
        g2 = norm2_g[l][None]
        wgu = w_gu[l].astype(BF16)
        wd = w_down[l].astype(BF16)

        q, k_all, kb, v_all, vb, gt, y, hl, cvn = _inproj_prompt(
            xp, g1, win, cos_p, sin_p, gq, gk, cw, cb, wax, ba, bx, lam_l, n_batch, t_pad, t_real,
            l, depth, kv_all)
        kv_all = (k_all, v_all)
        o = _pattn(q, kb, vb, lq, sg, n_batch, t_pad, t_real, lam_init)
        xp = _mlp(o, y, gt, xp, woa, wol, wout, g2, wgu, wd, t_pad // TOK_TILE)
        outs["hp"].append(hl[:, SUBLANES - 1])
        outs["cp"].append(cvn[:, SUBLANES - (CONV_W - 1):])

        q, k, v, xr, gr, gt = _inproj_sample(xs, g1, win, cos_s, sin_s, gq, gk)
        q6 = q.reshape(db, dec_seq, N_KV_HEADS, GROUP, 2, HEAD_DIM)
        qr = jnp.transpose(q6, (0, 2, 4, 3, 1, 5)).reshape(db, N_KV_HEADS, 2, GROUP * dec_seq, HEAD_DIM)
        qbd = jnp.einsum("bhcrd,hx,cy->bhcrxyd", qr, eye_h, eye_c).reshape(db, _DEC_ROWS, KV_W)
        knew_t = jnp.pad(jnp.transpose(k.reshape(db, dec_seq, KV_W), (0, 2, 1)),
                         ((0, 0), (0, 0), (0, page - dec_seq)))
        vnew2 = jnp.pad(v.reshape(db, dec_seq * N_KV_HEADS, 2 * HEAD_DIM),
                        ((0, 0), (0, (page - dec_seq) * N_KV_HEADS), (0, 0)))
        od = _dattn(page_table, qbd, ckt, cv2, knew_t, vnew2, lq, sg, l, lam_init)
        o = jnp.transpose(od.reshape(db, N_KV_HEADS, GROUP, dec_seq, 2 * HEAD_DIM),
                          (0, 3, 1, 2, 4)).reshape(db * dec_seq, Q_W).astype(BF16)
        conv_ext = jnp.concatenate([state_conv[l], jnp.zeros((db, dec_seq, D_RNN), F32)], axis=1)
        prevs = [conv_ext[:, CONV_W - 1 - j:CONV_W - 1 - j + dec_seq].reshape(db * dec_seq, D_RNN)
                 for j in range(1, CONV_W)]
        h0 = jnp.repeat(state_h[l], dec_seq, axis=0)
        y, hs = _lru_sample(xr, gr, prevs, h0, cw, cb, wax, ba, bx, lam_l, dec_seq)
        xs = _mlp(o, y, gt, xs, woa, wol, wout, g2, wgu, wd)
        outs["ks"].append(k.reshape(db, dec_seq, N_KV_HEADS, 2, HEAD_DIM))
        outs["vs"].append(v.reshape(db, dec_seq, N_KV_HEADS, 2 * HEAD_DIM))
        outs["hs"].append(hs.reshape(db, dec_seq, D_RNN)[:, -1])
        outs["cs"].append(xr.reshape(db, dec_seq, D_RNN)[:, dec_seq - (CONV_W - 1):])

    y_prompt = xp.reshape(n_batch, t_pad, D_MODEL)[:, N_META:t_real]
    y_sample = xs.reshape(db, dec_seq, D_MODEL)
    k_prompt = jnp.transpose(kv_all[0].reshape(depth, n_batch, N_KV_HEADS, 2, HEAD_DIM, t_real),
                             (0, 1, 5, 2, 3, 4))
    v_prompt = kv_all[1].reshape(depth, n_batch, t_real, N_KV_HEADS, 2 * HEAD_DIM)
    return (y_prompt, y_sample, k_prompt, v_prompt, jnp.stack(outs["hp"]), jnp.stack(outs["cp"]),
            jnp.stack(outs["ks"]), jnp.stack(outs["vs"]), jnp.stack(outs["hs"]), jnp.stack(outs["cs"]))
```
